```python
import jax, jax.numpy as jnp
from jax import lax
import numpy as np

D_MODEL = 2048
BATCH = 16
SEQ = 2048
DEPTH = 2

GRID_W = 64
CTX_LEN = 256

D_MIX = D_MODEL
HEAD_DIM = 128
N_HEADS = 8
N_KV_HEADS = 2
GQA_GROUP = N_HEADS // N_KV_HEADS
D_ATTN = N_HEADS * HEAD_DIM
D_KV = N_KV_HEADS * HEAD_DIM
D_SCONV = D_MIX // 4
D_CONF = D_MIX - D_ATTN - D_SCONV
SCONV_WIDTH = 3
CONF_WIDTH = 31
Q_BLOCK = 128
ROPE_THETA = 10000.0

D_IN_PROJ = D_ATTN + 2 * D_KV + 3 * D_SCONV + 2 * D_CONF
IN_SPLITS = (
    D_ATTN,
    D_ATTN + D_KV,
    D_ATTN + 2 * D_KV,
    D_ATTN + 2 * D_KV + D_SCONV,
    D_ATTN + 2 * D_KV + 2 * D_SCONV,
    D_ATTN + 2 * D_KV + 3 * D_SCONV,
    D_ATTN + 2 * D_KV + 3 * D_SCONV + D_CONF,
)

N_GROUPS = 4
EXPERTS_PER_GROUP = 8
N_EXPERTS = N_GROUPS * EXPERTS_PER_GROUP
TOP_K = 2
D_EXPERT = D_MODEL // 4
EXPERT_BLOCK = 256

N_MOD = 6
EPS = 1e-6

kernel_name = "hybrid_parallel_groups_dit_hmoe"


def rms_norm(x, g):
    xf = x.astype(jnp.float32)
    y = xf * lax.rsqrt(jnp.mean(jnp.square(xf), axis=-1, keepdims=True) + EPS)
    return y.astype(x.dtype) * g


def layer_norm(x, g, b):
    xf = x.astype(jnp.float32)
    mu = jnp.mean(xf, axis=-1, keepdims=True)
    var = jnp.mean(jnp.square(xf - mu), axis=-1, keepdims=True)
    return ((xf - mu) * lax.rsqrt(var + EPS)).astype(x.dtype) * g + b


def modulate(h, shift, scale):
    return h * (1 + scale) + shift


def heads(t, n):
    return t.reshape(t.shape[:-1] + (n, HEAD_DIM))


def axial_rope_tables(rows):
    row = jnp.broadcast_to(jnp.arange(rows, dtype=jnp.int32)[:, None], (rows, GRID_W)).reshape(-1)
    col = jnp.broadcast_to(jnp.arange(GRID_W, dtype=jnp.int32)[None, :], (rows, GRID_W)).reshape(-1)
    axis_dim = HEAD_DIM // 2
    inv_freq = ROPE_THETA ** (-jnp.arange(0, axis_dim, 2, dtype=jnp.float32) / axis_dim)
    ang_r = row.astype(jnp.float32)[:, None] * inv_freq
    ang_c = col.astype(jnp.float32)[:, None] * inv_freq
    ang = jnp.concatenate([ang_r, ang_r, ang_c, ang_c], axis=-1)
    return jnp.cos(ang), jnp.sin(ang)


def apply_axial_rope(t, cos, sin):
    tq = t.reshape(t.shape[:-1] + (2, 2, HEAD_DIM // 4))
    rot = jnp.stack([-tq[..., 1, :], tq[..., 0, :]], axis=-2).reshape(t.shape)
    out = t.astype(jnp.float32) * cos[:, None, :] + rot.astype(jnp.float32) * sin[:, None, :]
    return out.astype(t.dtype)


def depthwise_conv(u, w):
    k = w.shape[0]
    pad = (k - 1) // 2
    return lax.conv_general_dilated(
        u, w[:, None, :].astype(u.dtype), window_strides=(1,), padding=((pad, pad),),
        dimension_numbers=("NWC", "WIO", "NWC"), feature_group_count=u.shape[-1])


def attend(q, k, v):
    s = jnp.einsum("bqkgd,bskd->bkgqs", q, k, preferred_element_type=jnp.float32) * HEAD_DIM ** -0.5
    p = jax.nn.softmax(s, axis=-1).astype(v.dtype)
    return jnp.einsum("bkgqs,bskd->bqkgd", p, v)


def latent_attention(q, k, v):
    b, s = q.shape[:2]
    nb = s // Q_BLOCK
    qb = q.reshape(b, nb, Q_BLOCK, N_KV_HEADS, GQA_GROUP, HEAD_DIM).transpose(1, 0, 2, 3, 4, 5)
    ob = lax.map(lambda qblk: attend(qblk, k, v), qb)
    return ob.transpose(1, 0, 2, 3, 4, 5).reshape(b, s, D_ATTN)


def context_attention(q, k, v):
    b, l = q.shape[:2]
    return attend(q.reshape(b, l, N_KV_HEADS, GQA_GROUP, HEAD_DIM), k, v).reshape(b, l, D_ATTN)


def short_conv_group(gate_b, gate_c, u, w):
    return gate_b * depthwise_conv(gate_c * u, w)


def conformer_group(a, g, w, b, ln_g, ln_b):
    u = a * jax.nn.sigmoid(g)
    u = depthwise_conv(u, w) + b
    return jax.nn.silu(layer_norm(u, ln_g, ln_b))


def project_groups(h, w_in, q_g, k_g, sconv_w, dw_w, dw_b, ln_g, ln_b):
    p = h @ w_in
    q, k, v, gb, gc, u, ca, cg = jnp.split(p, IN_SPLITS, axis=-1)
    q = rms_norm(heads(q, N_HEADS), q_g)
    k = rms_norm(heads(k, N_KV_HEADS), k_g)
    v = heads(v, N_KV_HEADS)
    y_sconv = short_conv_group(gb, gc, u, sconv_w)
    y_conf = conformer_group(ca, cg, dw_w, dw_b, ln_g, ln_b)
    return q, k, v, y_sconv, y_conf


def merge_groups(y_attn, y_sconv, y_conf, g, w_out):
    y = jnp.concatenate([
        rms_norm(y_attn, g[:D_ATTN]),
        rms_norm(y_sconv, g[D_ATTN:D_ATTN + D_SCONV]),
        rms_norm(y_conf, g[D_ATTN + D_SCONV:]),
    ], axis=-1)
    return y @ w_out


def swiglu_expert(h, w_gate, w_up, w_down):
    return (jax.nn.silu(h @ w_gate) * (h @ w_up)) @ w_down


def hierarchical_moe(h, wg, bg, we, be, w_gate, w_up, w_down):
    n_tok, d = h.shape
    g_prob = jax.nn.softmax((h @ wg).astype(jnp.float32) + bg, axis=-1)
    g_top, g_idx = lax.top_k(g_prob, 1)
    e_logits = ((h @ we).astype(jnp.float32) + be).reshape(n_tok, N_GROUPS, EXPERTS_PER_GROUP)
    e_logits = jnp.take_along_axis(e_logits, g_idx[:, :, None], axis=1)[:, 0]
    e_top, e_idx = lax.top_k(jax.nn.softmax(e_logits, axis=-1), TOP_K)
    weight = (g_top * e_top / jnp.sum(e_top, axis=-1, keepdims=True)).reshape(-1)
    expert = (g_idx * EXPERTS_PER_GROUP + e_idx).reshape(-1)
    token = jnp.repeat(jnp.arange(n_tok, dtype=jnp.int32), TOP_K)
    n_assign = expert.shape[0]
    order = jnp.argsort(expert)
    e_sorted = expert[order]
    counts = jnp.bincount(expert, length=N_EXPERTS)
    starts = jnp.cumsum(counts) - counts
    padded = (counts + EXPERT_BLOCK - 1) // EXPERT_BLOCK * EXPERT_BLOCK
    pad_end = jnp.cumsum(padded)
    dest = (pad_end - padded)[e_sorted] + jnp.arange(n_assign, dtype=jnp.int32) - starts[e_sorted]
    n_blocks = -(-n_assign // EXPERT_BLOCK) + N_EXPERTS
    n_rows = n_blocks * EXPERT_BLOCK
    row_tok = jnp.full((n_rows,), n_tok, jnp.int32).at[dest].set(token[order])
    row_w = jnp.zeros((n_rows,), jnp.float32).at[dest].set(weight[order])
    block_expert = jnp.minimum(
        jnp.searchsorted(pad_end, jnp.arange(n_blocks, dtype=jnp.int32) * EXPERT_BLOCK, side="right"),
        N_EXPERTS - 1)
    h_pad = jnp.concatenate([h, jnp.zeros((1, d), h.dtype)], axis=0)

    def run_block(args):
        tok_blk, w_blk, e = args
        y = swiglu_expert(h_pad[tok_blk], w_gate[e], w_up[e], w_down[e])
        return y * w_blk[:, None].astype(y.dtype)

    y = lax.map(run_block, (row_tok.reshape(n_blocks, EXPERT_BLOCK),
                            row_w.reshape(n_blocks, EXPERT_BLOCK), block_expert))
    out = jnp.zeros((n_tok + 1, d), h.dtype).at[row_tok].add(y.reshape(n_rows, d))
    return out[:n_tok]


def setup_inputs(seed: int = 0) -> dict:
    key = jax.random.key(seed)
    ks = jax.random.split(key, 26)
    f32 = jnp.float32

    def nrm(k, shape, s):
        return jax.random.normal(k, shape, f32) * s

    return {
        "x": nrm(ks[0], (BATCH, SEQ, D_MODEL), 1.0),
        "c": nrm(ks[1], (BATCH, D_MODEL), 1.0),
        "ctx": nrm(ks[2], (BATCH, CTX_LEN, D_MODEL), 1.0),
        "c_ctx": nrm(ks[3], (D_MODEL,), 1.0),
        "w_ada": nrm(ks[4], (DEPTH, D_MODEL, N_MOD * D_MODEL), 0.5 * D_MODEL ** -0.5),
        "b_ada": nrm(ks[5], (DEPTH, N_MOD * D_MODEL), 0.02),
        "norm1_g": 1 + nrm(ks[6], (DEPTH, D_MODEL), 0.02),
        "w_in": nrm(ks[7], (DEPTH, D_MODEL, D_IN_PROJ), D_MODEL ** -0.5),
        "q_norm_g": 1 + nrm(ks[8], (DEPTH, HEAD_DIM), 0.02),
        "k_norm_g": 1 + nrm(ks[9], (DEPTH, HEAD_DIM), 0.02),
        "sconv_w": nrm(ks[10], (DEPTH, SCONV_WIDTH, D_SCONV), SCONV_WIDTH ** -0.5),
        "conf_dw_w": nrm(ks[11], (DEPTH, CONF_WIDTH, D_CONF), CONF_WIDTH ** -0.5),
        "conf_dw_b": nrm(ks[12], (DEPTH, D_CONF), 0.02),
        "conf_ln_g": 1 + nrm(ks[13], (DEPTH, D_CONF), 0.02),
        "conf_ln_b": nrm(ks[14], (DEPTH, D_CONF), 0.02),
        "grp_norm_g": 1 + nrm(ks[15], (DEPTH, D_MIX), 0.02),
        "w_out": nrm(ks[16], (DEPTH, D_MIX, D_MODEL), D_MIX ** -0.5),
        "norm2_g": 1 + nrm(ks[17], (DEPTH, D_MODEL), 0.02),
        "router_g_w": nrm(ks[18], (DEPTH, D_MODEL, N_GROUPS), D_MODEL ** -0.5),
        "router_g_b": nrm(ks[19], (DEPTH, N_GROUPS), 0.01),
        "router_e_w": nrm(ks[20], (DEPTH, D_MODEL, N_EXPERTS), D_MODEL ** -0.5),
        "router_e_b": nrm(ks[21], (DEPTH, N_EXPERTS), 0.01),
        "exp_w_gate": nrm(ks[22], (DEPTH, N_EXPERTS, D_MODEL, D_EXPERT), D_MODEL ** -0.5),
        "exp_w_up": nrm(ks[23], (DEPTH, N_EXPERTS, D_MODEL, D_EXPERT), D_MODEL ** -0.5),
        "exp_w_down": nrm(ks[24], (DEPTH, N_EXPERTS, D_EXPERT, D_MODEL), D_EXPERT ** -0.5),
        "final_g": 1 + nrm(ks[25], (D_MODEL,), 0.02),
    }


def reference(x, c, ctx, c_ctx, w_ada, b_ada, norm1_g, w_in, q_norm_g, k_norm_g, sconv_w,
              conf_dw_w, conf_dw_b, conf_ln_g, conf_ln_b, grp_norm_g, w_out, norm2_g,
              router_g_w, router_g_b, router_e_w, router_e_b, exp_w_gate, exp_w_up, exp_w_down,
              final_g):
    b, s, d = x.shape
    l = ctx.shape[1]
    rows = s // GRID_W
    cos, sin = axial_rope_tables(rows)
    c_act = jax.nn.silu(c)
    c_ctx_act = jax.nn.silu(c_ctx)
    for i in range(DEPTH):
        last = i == DEPTH - 1
        mod = (c_act @ w_ada[i] + b_ada[i])[:, None, :]
        sh1, sc1, gt1, sh2, sc2, gt2 = jnp.split(mod, N_MOD, axis=-1)
        mod_c = c_ctx_act @ w_ada[i] + b_ada[i]
        csh1, csc1, cgt1, csh2, csc2, cgt2 = jnp.split(mod_c, N_MOD, axis=-1)

        h_ctx = modulate(rms_norm(ctx, norm1_g[i]), csh1, csc1)
        if last:
            k_c, v_c = jnp.split(h_ctx @ w_in[i][:, D_ATTN:D_ATTN + 2 * D_KV], 2, axis=-1)
            k_c = rms_norm(heads(k_c, N_KV_HEADS), k_norm_g[i])
            v_c = heads(v_c, N_KV_HEADS)
        else:
            q_c, k_c, v_c, ysc_c, ycf_c = project_groups(
                h_ctx, w_in[i], q_norm_g[i], k_norm_g[i], sconv_w[i], conf_dw_w[i], conf_dw_b[i],
                conf_ln_g[i], conf_ln_b[i])
            mix_c = merge_groups(context_attention(q_c, k_c, v_c), ysc_c, ycf_c, grp_norm_g[i], w_out[i])
            ctx = ctx + cgt1 * mix_c

        h_lat = modulate(rms_norm(x, norm1_g[i]), sh1, sc1)
        q_l, k_l, v_l, ysc_l, ycf_l = project_groups(
            h_lat, w_in[i], q_norm_g[i], k_norm_g[i], sconv_w[i], conf_dw_w[i], conf_dw_b[i],
            conf_ln_g[i], conf_ln_b[i])
        q_l = apply_axial_rope(q_l, cos, sin)
        k_l = apply_axial_rope(k_l, cos, sin)
        att_l = latent_attention(q_l, jnp.concatenate([k_l, k_c], axis=1),
                                 jnp.concatenate([v_l, v_c], axis=1))
        x = x + gt1 * merge_groups(att_l, ysc_l, ycf_l, grp_norm_g[i], w_out[i])

        h2_lat = modulate(rms_norm(x, norm2_g[i]), sh2, sc2).reshape(b * s, d)
        if last:
            f = hierarchical_moe(h2_lat, router_g_w[i], router_g_b[i], router_e_w[i], router_e_b[i],
                                 exp_w_gate[i], exp_w_up[i], exp_w_down[i])
            x = x + gt2 * f.reshape(b, s, d)
        else:
            h2_ctx = modulate(rms_norm(ctx, norm2_g[i]), csh2, csc2).reshape(b * l, d)
            f = hierarchical_moe(jnp.concatenate([h2_lat, h2_ctx], axis=0), router_g_w[i], router_g_b[i],
                                 router_e_w[i], router_e_b[i], exp_w_gate[i], exp_w_up[i], exp_w_down[i])
            x = x + gt2 * f[: b * s].reshape(b, s, d)
            ctx = ctx + cgt2 * f[b * s:].reshape(b, l, d)
    return rms_norm(x, final_g)
```

```python
import functools

import jax
import jax.numpy as jnp
from jax import lax
from jax.experimental import pallas as pl
from jax.experimental.pallas import tpu as pltpu

F32 = jnp.float32
BF16 = jnp.bfloat16

HEAD_DIM = 128
N_HEADS = 8
N_KV_HEADS = 2
GQA_GROUP = N_HEADS // N_KV_HEADS
D_ATTN = N_HEADS * HEAD_DIM
D_KV = N_KV_HEADS * HEAD_DIM
GRID_W = 64
ROPE_THETA = 10000.0
SCONV_WIDTH = 3
CONF_WIDTH = 31
N_GROUPS = 4
EXPERTS_PER_GROUP = 8
N_EXPERTS = N_GROUPS * EXPERTS_PER_GROUP
TOP_K = 2
N_MOD = 6
EPS = 1e-6

MOD_ROWS = 32
ROUTER_LANES = 128
TM_PROJ = 512
TQ_ATTN = 256
CONV_ROWS = 256
CONV_HALO = 16
MOE_ROWS = 256
COMBINE_ROWS = 256
ADA_TN = 1024
VMEM_LIMIT = 56 * 1024 * 1024


def _cparams(*sem):
    return pltpu.CompilerParams(dimension_semantics=sem, vmem_limit_bytes=VMEM_LIMIT)


def _dot(a, b):
    return jnp.dot(a, b, preferred_element_type=F32)


def _rms(x, g):
    return x * lax.rsqrt(jnp.mean(x * x, axis=-1, keepdims=True) + EPS) * g


def _ada_kernel(c_ref, w_ref, b_ref, o_ref):
    a = c_ref[...]
    a = a * jax.nn.sigmoid(a)
    o_ref[0] = _dot(a.astype(BF16), w_ref[0].astype(BF16)) + b_ref[0]


def _ada_call(c_all, w_ada, b_ada):
    depth, d, n = w_ada.shape
    return pl.pallas_call(
        _ada_kernel,
        grid=(depth, n // ADA_TN),
        in_specs=[
            pl.BlockSpec((MOD_ROWS, d), lambda l, j: (0, 0)),
            pl.BlockSpec((1, d, ADA_TN), lambda l, j: (l, 0, j)),
            pl.BlockSpec((1, 1, ADA_TN), lambda l, j: (l, 0, j)),
        ],
        out_specs=pl.BlockSpec((1, MOD_ROWS, ADA_TN), lambda l, j: (l, 0, j)),
        out_shape=jax.ShapeDtypeStruct((depth, MOD_ROWS, n), F32),
        compiler_params=_cparams("arbitrary", "arbitrary"),
        name="ada_mod",
    )(c_all, w_ada, b_ada.reshape(depth, 1, n))


def _rope(t, cos, sin_lo, sin_hi):
    return t * cos + pltpu.roll(t, 96, 1) * sin_lo + pltpu.roll(t, 32, 1) * sin_hi


def _inproj_kernel(x_ref, mod_ref, g1_ref, w_ref, qg_ref, kg_ref, cos_ref, slo_ref, shi_ref,
                   q_ref, k_ref, v_ref, gb_ref, cu_ref, glu_ref):
    x = x_ref[...]
    shift = mod_ref[0, 0:1, :]
    scale = mod_ref[0, 1:2, :]
    h = _rms(x, g1_ref[...]) * (1.0 + scale) + shift
    hb = h.astype(BF16)
    cos, slo, shi = cos_ref[...], slo_ref[...], shi_ref[...]
    qg, kg = qg_ref[...], kg_ref[...]
    qscale = HEAD_DIM ** -0.5
    half = D_ATTN // 2
    for j in range(2):
        qc = _dot(hb, w_ref[:, j * half:(j + 1) * half])
        for hh in range(half // HEAD_DIM):
            t = _rms(qc[:, hh * HEAD_DIM:(hh + 1) * HEAD_DIM], qg)
            t = _rope(t, cos, slo, shi) * qscale
            q_ref[:, j * half + hh * HEAD_DIM: j * half + (hh + 1) * HEAD_DIM] = t.astype(BF16)
    kv = _dot(hb, w_ref[:, D_ATTN:D_ATTN + 2 * D_KV])
    for hh in range(N_KV_HEADS):
        t = _rms(kv[:, hh * HEAD_DIM:(hh + 1) * HEAD_DIM], kg)
        k_ref[:, hh * HEAD_DIM:(hh + 1) * HEAD_DIM] = _rope(t, cos, slo, shi).astype(BF16)
    v_ref[...] = kv[:, D_KV:].astype(BF16)
    c0 = D_ATTN + 2 * D_KV
    dg = gb_ref.shape[1]
    gb_ref[...] = _dot(hb, w_ref[:, c0:c0 + dg])
    cu_ref[...] = _dot(hb, w_ref[:, c0 + dg:c0 + 2 * dg]) * _dot(hb, w_ref[:, c0 + 2 * dg:c0 + 3 * dg])
    ca = _dot(hb, w_ref[:, c0 + 3 * dg:c0 + 4 * dg])
    cg = _dot(hb, w_ref[:, c0 + 4 * dg:c0 + 5 * dg])
    glu_ref[...] = ca * jax.nn.sigmoid(cg)


def _inproj_call(x_all, mod, g1, w_in_bf, qg, kg, cos, slo, shi, *, n_lat, seq):
    t_all, d = x_all.shape
    n_in = w_in_bf.shape[1]
    tm = TM_PROJ
    dg = (n_in - D_ATTN - 2 * D_KV) // 5
    lat_tiles = n_lat // tm
    seq_tiles = seq // tm
    n_batch = n_lat // seq

    def mod_map(i):
        return (jnp.minimum(i * tm // seq, n_batch), 0, 0)

    def rope_map(i):
        return (jnp.where(i < lat_tiles, i % seq_tiles, seq_tiles), 0)

    row = lambda i: (i, 0)
    fix = lambda i: (0, 0)
    return pl.pallas_call(
        _inproj_kernel,
        grid=(t_all // tm,),
        in_specs=[
            pl.BlockSpec((tm, d), row),
            pl.BlockSpec((1, N_MOD, d), mod_map),
            pl.BlockSpec((1, d), fix),
            pl.BlockSpec((d, n_in), fix, pipeline_mode=pl.Buffered(1)),
            pl.BlockSpec((1, HEAD_DIM), fix),
            pl.BlockSpec((1, HEAD_DIM), fix),
            pl.BlockSpec((tm, HEAD_DIM), rope_map),
            pl.BlockSpec((tm, HEAD_DIM), rope_map),
            pl.BlockSpec((tm, HEAD_DIM), rope_map),
        ],
        out_specs=[
            pl.BlockSpec((tm, D_ATTN), row),
            pl.BlockSpec((tm, D_KV), row),
            pl.BlockSpec((tm, D_KV), row),
            pl.BlockSpec((tm, dg), row),
            pl.BlockSpec((tm, dg), row),
            pl.BlockSpec((tm, dg), row),
        ],
        out_shape=[
            jax.ShapeDtypeStruct((t_all, D_ATTN), BF16),
            jax.ShapeDtypeStruct((t_all, D_KV), BF16),
            jax.ShapeDtypeStruct((t_all, D_KV), BF16),
            jax.ShapeDtypeStruct((t_all, dg), F32),
            jax.ShapeDtypeStruct((t_all, dg), F32),
            jax.ShapeDtypeStruct((t_all, dg), F32),
        ],
        compiler_params=_cparams("arbitrary"),
        name="in_proj",
    )(x_all, mod, g1, w_in_bf, qg, kg, cos, slo, shi)


def _attn_kernel(*refs, n_pieces):
    q_ref = refs[0]
    kv_refs = refs[1:1 + 2 * n_pieces]
    g_ref = refs[1 + 2 * n_pieces]
    o_ref = refs[2 + 2 * n_pieces]
    outs = []
    for hh in range(N_HEADS):
        kvh = hh // GQA_GROUP
        q = q_ref[:, hh * HEAD_DIM:(hh + 1) * HEAD_DIM]
        ks = [kv_refs[2 * p][:, kvh * HEAD_DIM:(kvh + 1) * HEAD_DIM] for p in range(n_pieces)]
        vs = [kv_refs[2 * p + 1][:, kvh * HEAD_DIM:(kvh + 1) * HEAD_DIM] for p in range(n_pieces)]
        ss = [lax.dot_general(q, k, (((1,), (1,)), ((), ())), preferred_element_type=F32) for k in ks]
        m = ss[0].max(axis=-1, keepdims=True)
        for s in ss[1:]:
            m = jnp.maximum(m, s.max(axis=-1, keepdims=True))
        ps = [jnp.exp(s - m) for s in ss]
        den = ps[0].sum(axis=-1, keepdims=True)
        for p in ps[1:]:
            den = den + p.sum(axis=-1, keepdims=True)
        o = _dot(ps[0].astype(BF16), vs[0])
        for p, v in zip(ps[1:], vs[1:]):
            o = o + _dot(p.astype(BF16), v)
        outs.append(o / den)
    y = jnp.concatenate(outs, axis=-1)
    o_ref[...] = _rms(y, g_ref[...]).astype(BF16)


def _attn_call(q, k, v, g_attn, *, n_q, q_off, seq_q, pieces):
    tq = min(TQ_ATTN, seq_q)
    n_batch = n_q // seq_q
    qt = seq_q // tq
    in_specs = [pl.BlockSpec((tq, D_ATTN), lambda b, i: (q_off // tq + b * qt + i, 0))]
    args = [q]
    for off, rows in pieces:
        for arr in (k, v):
            in_specs.append(pl.BlockSpec((rows, D_KV), functools.partial(
                lambda b, i, o, r: (o // r + b, 0), o=off, r=rows)))
            args.append(arr)
    in_specs.append(pl.BlockSpec((1, D_ATTN), lambda b, i: (0, 0)))
    args.append(g_attn)
    return pl.pallas_call(
        functools.partial(_attn_kernel, n_pieces=len(pieces)),
        grid=(n_batch, qt),
        in_specs=in_specs,
        out_specs=pl.BlockSpec((tq, D_ATTN), lambda b, i: (b * qt + i, 0)),
        out_shape=jax.ShapeDtypeStruct((n_q, D_ATTN), BF16),
        compiler_params=_cparams("arbitrary", "arbitrary"),
        name="attention",
    )(*args)


def _conv_kernel(gb_ref, cu_ref, cu_p_ref, cu_n_ref, gl_ref, gl_p_ref, gl_n_ref,
                 sw_ref, dw_ref, db_ref, lg_ref, lb_ref, gs_ref, gc_ref,
                 ys_ref, yc_ref, win_s, win_c, *, lat_chunks, seq_chunks, ctx_chunks):
    i = pl.program_id(0)
    r = CONV_ROWS
    hl = CONV_HALO
    pos = jnp.where(i < lat_chunks, i % seq_chunks, (i - lat_chunks) % ctx_chunks)
    last = jnp.where(i < lat_chunks, seq_chunks - 1, ctx_chunks - 1)
    keep_p = (pos > 0).astype(F32)
    keep_n = (pos < last).astype(F32)

    win_s[0:hl, :] = cu_p_ref[...] * keep_p
    win_s[hl:hl + r, :] = cu_ref[...]
    win_s[hl + r:, :] = cu_n_ref[...] * keep_n
    win_c[0:hl, :] = gl_p_ref[...] * keep_p
    win_c[hl:hl + r, :] = gl_ref[...]
    win_c[hl + r:, :] = gl_n_ref[...] * keep_n

    pad_s = (SCONV_WIDTH - 1) // 2
    acc = sw_ref[0:1, :] * win_s[hl - pad_s:hl - pad_s + r, :]
    for t in range(1, SCONV_WIDTH):
        acc = acc + sw_ref[t:t + 1, :] * win_s[hl - pad_s + t:hl - pad_s + t + r, :]
    ys_ref[...] = _rms(gb_ref[...] * acc, gs_ref[...]).astype(BF16)

    pad_c = (CONF_WIDTH - 1) // 2
    acc = dw_ref[0:1, :] * win_c[hl - pad_c:hl - pad_c + r, :]
    for t in range(1, CONF_WIDTH):
        acc = acc + dw_ref[t:t + 1, :] * win_c[hl - pad_c + t:hl - pad_c + t + r, :]
    u = acc + db_ref[...]
    mu = jnp.mean(u, axis=-1, keepdims=True)
    uc = u - mu
    var = jnp.mean(uc * uc, axis=-1, keepdims=True)
    z = uc * lax.rsqrt(var + EPS) * lg_ref[...] + lb_ref[...]
    z = z * jax.nn.sigmoid(z)
    yc_ref[...] = _rms(z, gc_ref[...]).astype(BF16)


def _conv_call(gb, cu, glu, sconv_w, dw_w, dw_b, ln_g, ln_b, g_s, g_c, *, n_rows, n_lat, seq, ctx_len):
    dg = gb.shape[1]
    r, hl = CONV_ROWS, CONV_HALO
    n_chunks = n_rows // r
    per = r // hl
    n_halo = n_rows // hl
    cur = lambda i: (i, 0)
    prev = lambda i: (jnp.maximum(i * per - 1, 0), 0)
    nxt = lambda i: (jnp.minimum((i + 1) * per, n_halo - 1), 0)
    fix = lambda i: (0, 0)
    kern = functools.partial(_conv_kernel, lat_chunks=n_lat // r, seq_chunks=seq // r,
                             ctx_chunks=max(ctx_len // r, 1))
    return pl.pallas_call(
        kern,
        grid=(n_chunks,),
        in_specs=[
            pl.BlockSpec((r, dg), cur),
            pl.BlockSpec((r, dg), cur), pl.BlockSpec((hl, dg), prev), pl.BlockSpec((hl, dg), nxt),
            pl.BlockSpec((r, dg), cur), pl.BlockSpec((hl, dg), prev), pl.BlockSpec((hl, dg), nxt),
            pl.BlockSpec((SCONV_WIDTH, dg), fix),
            pl.BlockSpec((CONF_WIDTH, dg), fix),
            pl.BlockSpec((1, dg), fix), pl.BlockSpec((1, dg), fix), pl.BlockSpec((1, dg), fix),
            pl.BlockSpec((1, dg), fix), pl.BlockSpec((1, dg), fix),
        ],
        out_specs=[pl.BlockSpec((r, dg), cur), pl.BlockSpec((r, dg), cur)],
        out_shape=[jax.ShapeDtypeStruct((n_rows, dg), BF16), jax.ShapeDtypeStruct((n_rows, dg), BF16)],
        scratch_shapes=[pltpu.VMEM((r + 2 * hl, dg), F32), pltpu.VMEM((r + 2 * hl, dg), F32)],
        compiler_params=_cparams("arbitrary"),
        name="group_convs",
    )(gb, cu, cu, cu, glu, glu, glu, sconv_w, dw_w, dw_b, ln_g, ln_b, g_s, g_c)


def _outproj_kernel(ya_ref, ys_ref, yc_ref, x_ref, mod_ref, wo_ref, g2_ref, wr_ref, br_ref,
                    xo_ref, h2_ref, lg_ref):
    da = ya_ref.shape[1]
    ds_ = ys_ref.shape[1]
    mix = _dot(ya_ref[...], wo_ref[0:da, :])
    mix = mix + _dot(ys_ref[...], wo_ref[da:da + ds_, :])
    mix = mix + _dot(yc_ref[...], wo_ref[da + ds_:, :])
    x = x_ref[...] + mod_ref[0, 2:3, :] * mix
    xo_ref[...] = x
    h2 = _rms(x, g2_ref[...]) * (1.0 + mod_ref[0, 4:5, :]) + mod_ref[0, 3:4, :]
    h2_ref[...] = h2
    lg_ref[...] = _dot(h2.astype(BF16), wr_ref[...]) + br_ref[...]


def _outproj_call(ya, ys, yc, x_all, mod, w_out_bf, g2, w_router, b_router, *, n_rows, n_lat, seq):
    d = x_all.shape[1]
    tm = TM_PROJ
    n_batch = n_lat // seq
    row = lambda i: (i, 0)
    fix = lambda i: (0, 0)
    mod_map = lambda i: (jnp.minimum(i * tm // seq, n_batch), 0, 0)
    return pl.pallas_call(
        _outproj_kernel,
        grid=(n_rows // tm,),
        in_specs=[
            pl.BlockSpec((tm, ya.shape[1]), row),
            pl.BlockSpec((tm, ys.shape[1]), row),
            pl.BlockSpec((tm, yc.shape[1]), row),
            pl.BlockSpec((tm, d), row),
            pl.BlockSpec((1, N_MOD, d), mod_map),
            pl.BlockSpec(w_out_bf.shape, fix, pipeline_mode=pl.Buffered(1)),
            pl.BlockSpec((1, d), fix),
            pl.BlockSpec((d, ROUTER_LANES), fix),
            pl.BlockSpec((1, ROUTER_LANES), fix),
        ],
        out_specs=[pl.BlockSpec((tm, d), row), pl.BlockSpec((tm, d), row),
                   pl.BlockSpec((tm, ROUTER_LANES), row)],
        out_shape=[jax.ShapeDtypeStruct((n_rows, d), F32), jax.ShapeDtypeStruct((n_rows, d), F32),
                   jax.ShapeDtypeStruct((n_rows, ROUTER_LANES), F32)],
        compiler_params=_cparams("arbitrary"),
        name="out_proj",
    )(ya, ys, yc, x_all, mod, w_out_bf, g2, w_router, b_router)


def _row_gather(src_hbm, idx_ref, base, n, dst, sem):
    def body(r, carry):
        tok = idx_ref[base + r]
        pltpu.make_async_copy(src_hbm.at[pl.ds(tok, 1), :], dst.at[pl.ds(r, 1), :], sem).start()
        return carry
    lax.fori_loop(0, n, body, 0)


def _row_gather_wait(src_hbm, n, dst, sem):
    pltpu.make_async_copy(src_hbm.at[pl.ds(0, n), :], dst, sem).wait()


def _moe_kernel(be_ref, tok_ref, used_ref, h_hbm, wg_ref, wu_ref, wd_ref, rw_ref, y_ref,
                xbuf, wgb, wub, wdb, sem):
    b = pl.program_id(0)
    nb = pl.num_programs(0)
    used = used_ref[0]
    slot = b % 2
    rows = MOE_ROWS

    @pl.when(b == 0)
    def _():
        _row_gather(h_hbm, tok_ref, 0, rows, xbuf.at[0], sem.at[0])

    @pl.when(jnp.logical_and(b + 1 < nb, b + 1 < used))
    def _():
        _row_gather(h_hbm, tok_ref, (b + 1) * rows, rows, xbuf.at[1 - slot], sem.at[1 - slot])

    @pl.when(b < used)
    def _():
        prev = be_ref[jnp.maximum(b - 1, 0)]

        @pl.when(jnp.logical_or(b == 0, be_ref[b] != prev))
        def _():
            wgb[...] = wg_ref[0].astype(BF16)
            wub[...] = wu_ref[0].astype(BF16)
            wdb[...] = wd_ref[0].astype(BF16)

        _row_gather_wait(h_hbm, rows, xbuf.at[slot], sem.at[slot])
        xb = xbuf[slot].astype(BF16)
        gate = _dot(xb, wgb[...])
        up = _dot(xb, wub[...])
        act = (gate * jax.nn.sigmoid(gate) * up).astype(BF16)
        y_ref[...] = _dot(act, wdb[...]) * rw_ref[...]

    @pl.when(b >= used)
    def _():
        y_ref[...] = jnp.zeros_like(y_ref)


def _moe_call(h2, block_expert, row_tok, n_used, row_w, w_gate, w_up, w_down):
    n_blocks = block_expert.shape[0]
    rows = MOE_ROWS
    n_exp, d, de = w_gate.shape

    def blk(b, be, tok, used):
        return jnp.minimum(b, used[0] - 1)

    grid_spec = pltpu.PrefetchScalarGridSpec(
        num_scalar_prefetch=3,
        grid=(n_blocks,),
        in_specs=[
            pl.BlockSpec(memory_space=pl.ANY),
            pl.BlockSpec((1, d, de), lambda b, be, tok, used: (be[blk(b, be, tok, used)], 0, 0)),
            pl.BlockSpec((1, d, de), lambda b, be, tok, used: (be[blk(b, be, tok, used)], 0, 0)),
            pl.BlockSpec((1, de, d), lambda b, be, tok, used: (be[blk(b, be, tok, used)], 0, 0)),
            pl.BlockSpec((rows, 1), lambda b, be, tok, used: (b, 0)),
        ],
        out_specs=pl.BlockSpec((rows, d), lambda b, be, tok, used: (b, 0)),
        scratch_shapes=[
            pltpu.VMEM((2, rows, d), F32),
            pltpu.VMEM((d, de), BF16),
            pltpu.VMEM((d, de), BF16),
            pltpu.VMEM((de, d), BF16),
            pltpu.SemaphoreType.DMA((2,)),
        ],
    )
    return pl.pallas_call(
        _moe_kernel,
        grid_spec=grid_spec,
        out_shape=jax.ShapeDtypeStruct((n_blocks * rows, d), F32),
        compiler_params=_cparams("arbitrary"),
        name="moe_experts",
    )(block_expert, row_tok, n_used, h2, w_gate, w_up, w_down, row_w)


def _combine_kernel(dst_ref, y_hbm, x_ref, mod_ref, g_ref, o_ref, ybuf, sem, *, final):
    i = pl.program_id(0)
    n = pl.num_programs(0)
    rows = COMBINE_ROWS
    slot = i % 2

    @pl.when(i == 0)
    def _():
        _row_gather(y_hbm, dst_ref, 0, TOP_K * rows, ybuf.at[0], sem.at[0])

    @pl.when(i + 1 < n)
    def _():
        _row_gather(y_hbm, dst_ref, (i + 1) * TOP_K * rows, TOP_K * rows, ybuf.at[1 - slot], sem.at[1 - slot])

    _row_gather_wait(y_hbm, TOP_K * rows, ybuf.at[slot], sem.at[slot])
    f = ybuf[slot, 0:rows, :]
    for t in range(1, TOP_K):
        f = f + ybuf[slot, t * rows:(t + 1) * rows, :]
    x = x_ref[...] + mod_ref[0, 5:6, :] * f
    if final:
        x = _rms(x, g_ref[...])
    o_ref[...] = x


def _combine_call(dest_tiles, y_sorted, x_all, mod, g_final, *, n_rows, n_lat, seq, final):
    d = x_all.shape[1]
    rows = COMBINE_ROWS
    n_batch = n_lat // seq
    grid_spec = pltpu.PrefetchScalarGridSpec(
        num_scalar_prefetch=1,
        grid=(n_rows // rows,),
        in_specs=[
            pl.BlockSpec(memory_space=pl.ANY),
            pl.BlockSpec((rows, d), lambda i, dst: (i, 0)),
            pl.BlockSpec((1, N_MOD, d), lambda i, dst: (jnp.minimum(i * rows // seq, n_batch), 0, 0)),
            pl.BlockSpec((1, d), lambda i, dst: (0, 0)),
        ],
        out_specs=pl.BlockSpec((rows, d), lambda i, dst: (i, 0)),
        scratch_shapes=[pltpu.VMEM((2, TOP_K * rows, d), F32), pltpu.SemaphoreType.DMA((2,))],
    )
    return pl.pallas_call(
        functools.partial(_combine_kernel, final=final),
        grid_spec=grid_spec,
        out_shape=jax.ShapeDtypeStruct((n_rows, d), F32),
        compiler_params=_cparams("arbitrary"),
        name="moe_combine",
    )(dest_tiles, y_sorted, x_all, mod, g_final)


def _route(logits, bias_g, bias_e):
    n_tok = logits.shape[0]
    g_prob = jax.nn.softmax(logits[:, :N_GROUPS], axis=-1)
    g_top, g_idx = lax.top_k(g_prob, 1)
    e_logits = logits[:, N_GROUPS:N_GROUPS + N_EXPERTS].reshape(n_tok, N_GROUPS, EXPERTS_PER_GROUP)
    e_logits = jnp.take_along_axis(e_logits, g_idx[:, :, None], axis=1)[:, 0]
    e_top, e_idx = lax.top_k(jax.nn.softmax(e_logits, axis=-1), TOP_K)
    weight = g_top * e_top / jnp.sum(e_top, axis=-1, keepdims=True)
    expert = (g_idx * EXPERTS_PER_GROUP + e_idx).astype(jnp.int32)

    rows = MOE_ROWS
    n_assign = n_tok * TOP_K
    n_blocks = n_assign // rows + N_EXPERTS
    flat_e = expert.reshape(-1)
    onehot = (flat_e[:, None] == jnp.arange(N_EXPERTS, dtype=jnp.int32)[None, :]).astype(jnp.int32)
    csum = jnp.cumsum(onehot, axis=0)
    counts = csum[-1]
    rank = jnp.take_along_axis(csum, flat_e[:, None], axis=1)[:, 0] - 1
    padded = (counts + rows - 1) // rows * rows
    pad_end = jnp.cumsum(padded)
    pad_start = pad_end - padded
    dest = pad_start[flat_e] + rank
    token = jnp.repeat(jnp.arange(n_tok, dtype=jnp.int32), TOP_K)
    row_tok = jnp.zeros((n_blocks * rows,), jnp.int32).at[dest].set(token, unique_indices=True)
    row_w = jnp.zeros((n_blocks * rows,), F32).at[dest].set(weight.reshape(-1), unique_indices=True)
    block_expert = jnp.minimum(
        jnp.searchsorted(pad_end, jnp.arange(n_blocks, dtype=jnp.int32) * rows, side="right"),
        N_EXPERTS - 1).astype(jnp.int32)
    n_used = (pad_end[-1] // rows).astype(jnp.int32).reshape(1)
    dest_tiles = dest.reshape(n_tok // COMBINE_ROWS, COMBINE_ROWS, TOP_K).transpose(0, 2, 1).reshape(-1)
    return block_expert, row_tok, n_used, row_w.reshape(-1, 1), dest_tiles.astype(jnp.int32)


def _rope_tables(seq, pad_rows):
    rows = seq // GRID_W
    row = jnp.broadcast_to(jnp.arange(rows, dtype=jnp.int32)[:, None], (rows, GRID_W)).reshape(-1)
    col = jnp.broadcast_to(jnp.arange(GRID_W, dtype=jnp.int32)[None, :], (rows, GRID_W)).reshape(-1)
    axis_dim = HEAD_DIM // 2
    inv_freq = ROPE_THETA ** (-jnp.arange(0, axis_dim, 2, dtype=F32) / axis_dim)
    ang_r = row.astype(F32)[:, None] * inv_freq
    ang_c = col.astype(F32)[:, None] * inv_freq
    ang = jnp.concatenate([ang_r, ang_r, ang_c, ang_c], axis=-1)
    cos, sin = jnp.cos(ang), jnp.sin(ang)
    lo = (jnp.arange(HEAD_DIM) % (HEAD_DIM // 2)) < (HEAD_DIM // 4)
    sin_lo = jnp.where(lo[None, :], -sin, 0.0)
    sin_hi = jnp.where(lo[None, :], 0.0, sin)
    cos = jnp.concatenate([cos, jnp.ones((pad_rows, HEAD_DIM), F32)], axis=0)
    zeros = jnp.zeros((pad_rows, HEAD_DIM), F32)
    return cos, jnp.concatenate([sin_lo, zeros], axis=0), jnp.concatenate([sin_hi, zeros], axis=0)


def kernel(x, c, ctx, c_ctx, w_ada, b_ada, norm1_g, w_in, q_norm_g, k_norm_g, sconv_w, conf_dw_w,
           conf_dw_b, conf_ln_g, conf_ln_b, grp_norm_g, w_out, norm2_g, router_g_w, router_g_b,
           router_e_w, router_e_b, exp_w_gate, exp_w_up, exp_w_down, final_g):
    n_batch, seq, d = x.shape
    ctx_len = ctx.shape[1]
    depth = w_ada.shape[0]
    n_lat = n_batch * seq
    n_ctx = n_batch * ctx_len
    n_all = n_lat + n_ctx
    d_sconv = sconv_w.shape[2]
    assert n_batch + 1 <= MOD_ROWS and seq % TM_PROJ == 0 and n_ctx % TM_PROJ == 0
    assert seq % CONV_ROWS == 0 and (ctx_len % CONV_ROWS == 0) and seq % GRID_W == 0
    assert n_lat % COMBINE_ROWS == 0 and n_ctx % COMBINE_ROWS == 0 and ctx_len % 8 == 0

    c_all = jnp.concatenate([c, c_ctx[None, :], jnp.zeros((MOD_ROWS - n_batch - 1, d), F32)], axis=0)
    mod_all = _ada_call(c_all, w_ada, b_ada).reshape(depth, MOD_ROWS, N_MOD, d)
    cos, sin_lo, sin_hi = _rope_tables(seq, TM_PROJ)
    x_all = jnp.concatenate([x.reshape(n_lat, d), ctx.reshape(n_ctx, d)], axis=0)

    for i in range(depth):
        last = i == depth - 1
        mod = mod_all[i]
        n_rows = n_lat if last else n_all
        row2 = lambda a: a.reshape(1, -1)
        q, k, v, gb, cu, glu = _inproj_call(
            x_all, mod, row2(norm1_g[i]), w_in[i].astype(BF16), row2(q_norm_g[i]), row2(k_norm_g[i]),
            cos, sin_lo, sin_hi, n_lat=n_lat, seq=seq)
        g_attn = row2(grp_norm_g[i][:D_ATTN])
        ya = _attn_call(q, k, v, g_attn, n_q=n_lat, q_off=0, seq_q=seq,
                        pieces=[(0, seq), (n_lat, ctx_len)])
        if not last:
            ya_c = _attn_call(q, k, v, g_attn, n_q=n_ctx, q_off=n_lat, seq_q=ctx_len,
                              pieces=[(n_lat, ctx_len)])
            ya = jnp.concatenate([ya, ya_c], axis=0)
        ys, yc = _conv_call(
            gb, cu, glu, sconv_w[i], conf_dw_w[i], row2(conf_dw_b[i]), row2(conf_ln_g[i]),
            row2(conf_ln_b[i]), row2(grp_norm_g[i][D_ATTN:D_ATTN + d_sconv]),
            row2(grp_norm_g[i][D_ATTN + d_sconv:]), n_rows=n_rows, n_lat=n_lat, seq=seq, ctx_len=ctx_len)
        w_router = jnp.concatenate(
            [router_g_w[i], router_e_w[i],
             jnp.zeros((d, ROUTER_LANES - N_GROUPS - N_EXPERTS), F32)], axis=1).astype(BF16)
        b_router = jnp.concatenate(
            [router_g_b[i], router_e_b[i], jnp.zeros((ROUTER_LANES - N_GROUPS - N_EXPERTS,), F32)])[None, :]
        x_mid, h2, logits = _outproj_call(
            ya, ys, yc, x_all, mod, w_out[i].astype(BF16), row2(norm2_g[i]), w_router, b_router,
            n_rows=n_rows, n_lat=n_lat, seq=seq)
        block_expert, row_tok, n_used, row_w, dest_tiles = _route(logits, None, None)
        y_sorted = _moe_call(h2, block_expert, row_tok, n_used, row_w,
                             exp_w_gate[i], exp_w_up[i], exp_w_down[i])
        x_all = _combine_call(dest_tiles, y_sorted, x_mid, mod, row2(final_g),
                              n_rows=n_rows, n_lat=n_lat, seq=seq, final=last)
    return x_all.reshape(n_batch, seq, d)
```

```python
import functools

import jax
import jax.numpy as jnp
from jax import lax
from jax.experimental import pallas as pl
from jax.experimental.pallas import tpu as pltpu

F32 = jnp.float32
BF16 = jnp.bfloat16
I32 = jnp.int32

HEAD_DIM = 128
N_HEADS = 8
N_KV_HEADS = 2
GQA_GROUP = N_HEADS // N_KV_HEADS
D_ATTN = N_HEADS * HEAD_DIM
D_KV = N_KV_HEADS * HEAD_DIM
GRID_W = 64
ROPE_THETA = 10000.0
SCONV_WIDTH = 3
CONF_WIDTH = 31
N_GROUPS = 4
EXPERTS_PER_GROUP = 8
N_EXPERTS = N_GROUPS * EXPERTS_PER_GROUP
TOP_K = 2
N_MOD = 6
EPS = 1e-6

MOD_ROWS = 32
ROUTER_LANES = 128
TM_PROJ = 512
TQ_ATTN = 256
CONV_ROWS = 256
CONV_HALO = 16
ROUTE_ROWS = 512
DISPATCH_ROWS = 512
MOE_ROWS = 256
COMBINE_ROWS = 256
ADA_TN = 1024
VMEM_LIMIT = 56 * 1024 * 1024


def _cparams(*sem):
    return pltpu.CompilerParams(dimension_semantics=sem, vmem_limit_bytes=VMEM_LIMIT)


def _dot(a, b):
    return jnp.dot(a, b, preferred_element_type=F32)


def _rms(x, g):
    return x * lax.rsqrt(jnp.mean(x * x, axis=-1, keepdims=True) + EPS) * g


def _ada_kernel(c_ref, w_ref, b_ref, o_ref):
    a = c_ref[...]
    a = a * jax.nn.sigmoid(a)
    o_ref[0] = _dot(a.astype(BF16), w_ref[0].astype(BF16)) + b_ref[0]


def _ada_call(c_all, w_ada, b_ada):
    depth, d, n = w_ada.shape
    return pl.pallas_call(
        _ada_kernel,
        grid=(depth, n // ADA_TN),
        in_specs=[
            pl.BlockSpec((MOD_ROWS, d), lambda l, j: (0, 0)),
            pl.BlockSpec((1, d, ADA_TN), lambda l, j: (l, 0, j)),
            pl.BlockSpec((1, 1, ADA_TN), lambda l, j: (l, 0, j)),
        ],
        out_specs=pl.BlockSpec((1, MOD_ROWS, ADA_TN), lambda l, j: (l, 0, j)),
        out_shape=jax.ShapeDtypeStruct((depth, MOD_ROWS, n), F32),
        compiler_params=_cparams("arbitrary", "arbitrary"),
        name="ada_mod",
    )(c_all, w_ada, b_ada.reshape(depth, 1, n))


def _rope(t, cos, sin_lo, sin_hi):
    return t * cos + pltpu.roll(t, 96, 1) * sin_lo + pltpu.roll(t, 32, 1) * sin_hi


def _inproj_kernel(xa_ref, xb_ref, mod_ref, g1_ref, w_ref, qg_ref, kg_ref, cos_ref, slo_ref, shi_ref,
                   q_ref, k_ref, v_ref, gb_ref, cu_ref, glu_ref, *, a_tiles):
    x = jnp.where(pl.program_id(0) < a_tiles, xa_ref[...], xb_ref[...])
    shift = mod_ref[0, 0:1, :]
    scale = mod_ref[0, 1:2, :]
    h = _rms(x, g1_ref[...]) * (1.0 + scale) + shift
    hb = h.astype(BF16)
    cos, slo, shi = cos_ref[...], slo_ref[...], shi_ref[...]
    qg, kg = qg_ref[...], kg_ref[...]
    qscale = HEAD_DIM ** -0.5
    half = D_ATTN // 2
    for j in range(2):
        qc = _dot(hb, w_ref[:, j * half:(j + 1) * half])
        for hh in range(half // HEAD_DIM):
            t = _rms(qc[:, hh * HEAD_DIM:(hh + 1) * HEAD_DIM], qg)
            t = _rope(t, cos, slo, shi) * qscale
            q_ref[:, j * half + hh * HEAD_DIM: j * half + (hh + 1) * HEAD_DIM] = t.astype(BF16)
    kv = _dot(hb, w_ref[:, D_ATTN:D_ATTN + 2 * D_KV])
    for hh in range(N_KV_HEADS):
        t = _rms(kv[:, hh * HEAD_DIM:(hh + 1) * HEAD_DIM], kg)
        k_ref[:, hh * HEAD_DIM:(hh + 1) * HEAD_DIM] = _rope(t, cos, slo, shi).astype(BF16)
    v_ref[...] = kv[:, D_KV:].astype(BF16)
    c0 = D_ATTN + 2 * D_KV
    dg = gb_ref.shape[1]
    gb_ref[...] = _dot(hb, w_ref[:, c0:c0 + dg])
    cu_ref[...] = _dot(hb, w_ref[:, c0 + dg:c0 + 2 * dg]) * _dot(hb, w_ref[:, c0 + 2 * dg:c0 + 3 * dg])
    ca = _dot(hb, w_ref[:, c0 + 3 * dg:c0 + 4 * dg])
    cg = _dot(hb, w_ref[:, c0 + 4 * dg:c0 + 5 * dg])
    glu_ref[...] = ca * jax.nn.sigmoid(cg)


def _two_source_specs(tm, d, a_tiles, b_off):
    return [pl.BlockSpec((tm, d), lambda i, *_: (jnp.minimum(i, a_tiles - 1), 0)),
            pl.BlockSpec((tm, d), lambda i, *_: (jnp.maximum(i - a_tiles, 0) + b_off, 0))]


def _inproj_call(xa, xb, b_off, mod, g1, w_in_bf, qg, kg, cos, slo, shi, *, n_all, n_lat, seq):
    d = xa.shape[1]
    n_in = w_in_bf.shape[1]
    tm = TM_PROJ
    dg = (n_in - D_ATTN - 2 * D_KV) // 5
    lat_tiles = n_lat // tm
    seq_tiles = seq // tm
    n_batch = n_lat // seq

    def mod_map(i):
        return (jnp.minimum(i * tm // seq, n_batch), 0, 0)

    def rope_map(i):
        return (jnp.where(i < lat_tiles, i % seq_tiles, seq_tiles), 0)

    row = lambda i: (i, 0)
    fix = lambda i: (0, 0)
    return pl.pallas_call(
        functools.partial(_inproj_kernel, a_tiles=lat_tiles),
        grid=(n_all // tm,),
        in_specs=_two_source_specs(tm, d, lat_tiles, b_off) + [
            pl.BlockSpec((1, N_MOD, d), mod_map),
            pl.BlockSpec((1, d), fix),
            pl.BlockSpec((d, n_in), fix, pipeline_mode=pl.Buffered(1)),
            pl.BlockSpec((1, HEAD_DIM), fix),
            pl.BlockSpec((1, HEAD_DIM), fix),
            pl.BlockSpec((tm, HEAD_DIM), rope_map),
            pl.BlockSpec((tm, HEAD_DIM), rope_map),
            pl.BlockSpec((tm, HEAD_DIM), rope_map),
        ],
        out_specs=[
            pl.BlockSpec((tm, D_ATTN), row),
            pl.BlockSpec((tm, D_KV), row),
            pl.BlockSpec((tm, D_KV), row),
            pl.BlockSpec((tm, dg), row),
            pl.BlockSpec((tm, dg), row),
            pl.BlockSpec((tm, dg), row),
        ],
        out_shape=[
            jax.ShapeDtypeStruct((n_all, D_ATTN), BF16),
            jax.ShapeDtypeStruct((n_all, D_KV), BF16),
            jax.ShapeDtypeStruct((n_all, D_KV), BF16),
            jax.ShapeDtypeStruct((n_all, dg), F32),
            jax.ShapeDtypeStruct((n_all, dg), F32),
            jax.ShapeDtypeStruct((n_all, dg), F32),
        ],
        compiler_params=_cparams("arbitrary"),
        name="in_proj",
    )(xa, xb, mod, g1, w_in_bf, qg, kg, cos, slo, shi)


def _attend(q_ref, kv_refs, g_ref, o_ref):
    outs = []
    for hh in range(N_HEADS):
        kvh = hh // GQA_GROUP
        q = q_ref[:, hh * HEAD_DIM:(hh + 1) * HEAD_DIM]
        ks = [k_ref[:, kvh * HEAD_DIM:(kvh + 1) * HEAD_DIM] for k_ref, _ in kv_refs]
        vs = [v_ref[:, kvh * HEAD_DIM:(kvh + 1) * HEAD_DIM] for _, v_ref in kv_refs]
        ss = [lax.dot_general(q, k, (((1,), (1,)), ((), ())), preferred_element_type=F32) for k in ks]
        m = ss[0].max(axis=-1, keepdims=True)
        for s in ss[1:]:
            m = jnp.maximum(m, s.max(axis=-1, keepdims=True))
        ps = [jnp.exp(s - m) for s in ss]
        den = ps[0].sum(axis=-1, keepdims=True)
        for p in ps[1:]:
            den = den + p.sum(axis=-1, keepdims=True)
        o = _dot(ps[0].astype(BF16), vs[0])
        for p, v in zip(ps[1:], vs[1:]):
            o = o + _dot(p.astype(BF16), v)
        outs.append(o / den)
    y = jnp.concatenate(outs, axis=-1)
    o_ref[...] = _rms(y, g_ref[...]).astype(BF16)


def _attn_kernel(q_ref, kl_ref, vl_ref, kc_ref, vc_ref, g_ref, o_ref, *, lat_steps, ctx_steps):
    if ctx_steps == 0:
        _attend(q_ref, [(kl_ref, vl_ref), (kc_ref, vc_ref)], g_ref, o_ref)
        return
    i = pl.program_id(1)

    @pl.when(i < lat_steps)
    def _():
        _attend(q_ref, [(kl_ref, vl_ref), (kc_ref, vc_ref)], g_ref, o_ref)

    @pl.when(i >= lat_steps)
    def _():
        _attend(q_ref, [(kc_ref, vc_ref)], g_ref, o_ref)


def _attn_call(q, k, v, g_attn, *, n_lat, seq, ctx_len, ctx_queries):
    tq = TQ_ATTN
    n_batch = n_lat // seq
    qt = seq // tq
    ct = ctx_len // tq if ctx_queries else 0
    n_out = n_lat + (n_batch * ctx_len if ctx_queries else 0)

    def q_map(b, i):
        return (jnp.where(i < qt, b * qt + i, n_lat // tq + b * ct + (i - qt)), 0)

    lat_kv = pl.BlockSpec((seq, D_KV), lambda b, i: (b, 0))
    ctx_kv = pl.BlockSpec((ctx_len, D_KV), lambda b, i: (n_lat // ctx_len + b, 0))
    return pl.pallas_call(
        functools.partial(_attn_kernel, lat_steps=qt, ctx_steps=ct),
        grid=(n_batch, qt + ct),
        in_specs=[pl.BlockSpec((tq, D_ATTN), q_map), lat_kv, lat_kv, ctx_kv, ctx_kv,
                  pl.BlockSpec((1, D_ATTN), lambda b, i: (0, 0))],
        out_specs=pl.BlockSpec((tq, D_ATTN), q_map),
        out_shape=jax.ShapeDtypeStruct((n_out, D_ATTN), BF16),
        compiler_params=_cparams("arbitrary", "arbitrary"),
        name="attention",
    )(q, k, v, k, v, g_attn)


def _conv_kernel(gb_ref, cu_ref, cu_p_ref, cu_n_ref, gl_ref, gl_p_ref, gl_n_ref,
                 sw_ref, dw_ref, db_ref, lg_ref, lb_ref, gs_ref, gc_ref,
                 ys_ref, yc_ref, win_s, win_c, *, lat_chunks, seq_chunks, ctx_chunks):
    i = pl.program_id(0)
    r = CONV_ROWS
    hl = CONV_HALO
    pos = jnp.where(i < lat_chunks, i % seq_chunks, (i - lat_chunks) % ctx_chunks)
    last = jnp.where(i < lat_chunks, seq_chunks - 1, ctx_chunks - 1)
    keep_p = (pos > 0).astype(F32)
    keep_n = (pos < last).astype(F32)

    win_s[0:hl, :] = cu_p_ref[...] * keep_p
    win_s[hl:hl + r, :] = cu_ref[...]
    win_s[hl + r:, :] = cu_n_ref[...] * keep_n
    win_c[0:hl, :] = gl_p_ref[...] * keep_p
    win_c[hl:hl + r, :] = gl_ref[...]
    win_c[hl + r:, :] = gl_n_ref[...] * keep_n

    pad_s = (SCONV_WIDTH - 1) // 2
    acc = sw_ref[0:1, :] * win_s[hl - pad_s:hl - pad_s + r, :]
    for t in range(1, SCONV_WIDTH):
        acc = acc + sw_ref[t:t + 1, :] * win_s[hl - pad_s + t:hl - pad_s + t + r, :]
    ys_ref[...] = _rms(gb_ref[...] * acc, gs_ref[...]).astype(BF16)

    pad_c = (CONF_WIDTH - 1) // 2
    acc = dw_ref[0:1, :] * win_c[hl - pad_c:hl - pad_c + r, :]
    for t in range(1, CONF_WIDTH):
        acc = acc + dw_ref[t:t + 1, :] * win_c[hl - pad_c + t:hl - pad_c + t + r, :]
    u = acc + db_ref[...]
    mu = jnp.mean(u, axis=-1, keepdims=True)
    uc = u - mu
    var = jnp.mean(uc * uc, axis=-1, keepdims=True)
    z = uc * lax.rsqrt(var + EPS) * lg_ref[...] + lb_ref[...]
    z = z * jax.nn.sigmoid(z)
    yc_ref[...] = _rms(z, gc_ref[...]).astype(BF16)


def _conv_call(gb, cu, glu, sconv_w, dw_w, dw_b, ln_g, ln_b, g_s, g_c, *, n_rows, n_lat, seq, ctx_len):
    dg = gb.shape[1]
    r, hl = CONV_ROWS, CONV_HALO
    n_chunks = n_rows // r
    per = r // hl
    n_halo = n_rows // hl
    cur = lambda i: (i, 0)
    prev = lambda i: (jnp.maximum(i * per - 1, 0), 0)
    nxt = lambda i: (jnp.minimum((i + 1) * per, n_halo - 1), 0)
    fix = lambda i: (0, 0)
    kern = functools.partial(_conv_kernel, lat_chunks=n_lat // r, seq_chunks=seq // r,
                             ctx_chunks=max(ctx_len // r, 1))
    return pl.pallas_call(
        kern,
        grid=(n_chunks,),
        in_specs=[
            pl.BlockSpec((r, dg), cur),
            pl.BlockSpec((r, dg), cur), pl.BlockSpec((hl, dg), prev), pl.BlockSpec((hl, dg), nxt),
            pl.BlockSpec((r, dg), cur), pl.BlockSpec((hl, dg), prev), pl.BlockSpec((hl, dg), nxt),
            pl.BlockSpec((SCONV_WIDTH, dg), fix),
            pl.BlockSpec((CONF_WIDTH, dg), fix),
            pl.BlockSpec((1, dg), fix), pl.BlockSpec((1, dg), fix), pl.BlockSpec((1, dg), fix),
            pl.BlockSpec((1, dg), fix), pl.BlockSpec((1, dg), fix),
        ],
        out_specs=[pl.BlockSpec((r, dg), cur), pl.BlockSpec((r, dg), cur)],
        out_shape=[jax.ShapeDtypeStruct((n_rows, dg), BF16), jax.ShapeDtypeStruct((n_rows, dg), BF16)],
        scratch_shapes=[pltpu.VMEM((r + 2 * hl, dg), F32), pltpu.VMEM((r + 2 * hl, dg), F32)],
        compiler_params=_cparams("arbitrary"),
        name="group_convs",
    )(gb, cu, cu, cu, glu, glu, glu, sconv_w, dw_w, dw_b, ln_g, ln_b, g_s, g_c)


def _outproj_kernel(ya_ref, ys_ref, yc_ref, xa_ref, xb_ref, mod_ref, wo_ref, g2_ref, wr_ref, br_ref,
                    xo_ref, h2_ref, lg_ref, *, a_tiles):
    da = ya_ref.shape[1]
    ds_ = ys_ref.shape[1]
    mix = _dot(ya_ref[...], wo_ref[0:da, :])
    mix = mix + _dot(ys_ref[...], wo_ref[da:da + ds_, :])
    mix = mix + _dot(yc_ref[...], wo_ref[da + ds_:, :])
    x = jnp.where(pl.program_id(0) < a_tiles, xa_ref[...], xb_ref[...])
    x = x + mod_ref[0, 2:3, :] * mix
    xo_ref[...] = x
    h2 = _rms(x, g2_ref[...]) * (1.0 + mod_ref[0, 4:5, :]) + mod_ref[0, 3:4, :]
    h2_ref[...] = h2
    lg_ref[...] = _dot(h2.astype(BF16), wr_ref[...]) + br_ref[...]


def _outproj_call(ya, ys, yc, xa, xb, b_off, mod, w_out_bf, g2, w_router, b_router, *, n_rows, n_lat, seq):
    d = xa.shape[1]
    tm = TM_PROJ
    n_batch = n_lat // seq
    row = lambda i: (i, 0)
    fix = lambda i: (0, 0)
    mod_map = lambda i: (jnp.minimum(i * tm // seq, n_batch), 0, 0)
    return pl.pallas_call(
        functools.partial(_outproj_kernel, a_tiles=n_lat // tm),
        grid=(n_rows // tm,),
        in_specs=[
            pl.BlockSpec((tm, ya.shape[1]), row),
            pl.BlockSpec((tm, ys.shape[1]), row),
            pl.BlockSpec((tm, yc.shape[1]), row),
        ] + _two_source_specs(tm, d, n_lat // tm, b_off) + [
            pl.BlockSpec((1, N_MOD, d), mod_map),
            pl.BlockSpec(w_out_bf.shape, fix, pipeline_mode=pl.Buffered(1)),
            pl.BlockSpec((1, d), fix),
            pl.BlockSpec((d, ROUTER_LANES), fix),
            pl.BlockSpec((1, ROUTER_LANES), fix),
        ],
        out_specs=[pl.BlockSpec((tm, d), row), pl.BlockSpec((tm, d), row),
                   pl.BlockSpec((tm, ROUTER_LANES), row)],
        out_shape=[jax.ShapeDtypeStruct((n_rows, d), F32), jax.ShapeDtypeStruct((n_rows, d), F32),
                   jax.ShapeDtypeStruct((n_rows, ROUTER_LANES), F32)],
        compiler_params=_cparams("arbitrary"),
        name="out_proj",
    )(ya, ys, yc, xa, xb, mod, w_out_bf, g2, w_router, b_router)


def _first_lane_of_max(vals, lane_f):
    m = vals.max(axis=-1, keepdims=True)
    return m, jnp.where(vals == m, lane_f, float(ROUTER_LANES)).min(axis=-1, keepdims=True)


def _route_kernel(lg_ref, idx_ref, wt_ref, cnt_ref, carry):
    i = pl.program_id(0)
    tr = lg_ref.shape[0]

    @pl.when(i == 0)
    def _():
        carry[...] = jnp.zeros_like(carry)

    lg = lg_ref[...]
    lane = lax.broadcasted_iota(I32, lg.shape, 1)
    lane_f = lane.astype(F32)
    neg = -jnp.inf
    gmask = lane < N_GROUPS
    gl = jnp.where(gmask, lg, neg)
    gmax, g_idx = _first_lane_of_max(gl, lane_f)
    g_top = 1.0 / jnp.where(gmask, jnp.exp(gl - gmax), 0.0).sum(axis=-1, keepdims=True)
    lo = float(N_GROUPS) + g_idx * float(EXPERTS_PER_GROUP)
    emask = jnp.logical_and(lane_f >= lo, lane_f < lo + float(EXPERTS_PER_GROUP))
    el = jnp.where(emask, lg, neg)
    m1, l1 = _first_lane_of_max(el, lane_f)
    o1 = lane_f == l1
    el2 = jnp.where(o1, neg, el)
    m2, l2 = _first_lane_of_max(el2, lane_f)
    o2 = lane_f == l2
    ratio = jnp.exp(m2 - m1)
    w1 = g_top / (1.0 + ratio)
    w2 = w1 * ratio

    onehot = jnp.where(jnp.logical_or(o1, o2), 1.0, 0.0)
    rows_i = lax.broadcasted_iota(I32, (tr, tr), 0)
    cols_i = lax.broadcasted_iota(I32, (tr, tr), 1)
    below = jnp.where(cols_i < rows_i, 1.0, 0.0).astype(BF16)
    base = carry[...] + _dot(below, onehot.astype(BF16))
    r1 = jnp.where(o1, base, 0.0).sum(axis=-1, keepdims=True)
    r2 = jnp.where(o2, base, 0.0).sum(axis=-1, keepdims=True)
    carry[...] = carry[...] + onehot.sum(axis=0, keepdims=True)
    cnt_ref[...] = jnp.broadcast_to(carry[...], cnt_ref.shape)

    e1 = l1 - float(N_GROUPS)
    e2 = l2 - float(N_GROUPS)
    idx = jnp.where(lane == 0, e1, jnp.where(lane == 1, e2, jnp.where(lane == 2, r1, r2)))
    idx_ref[...] = idx.astype(I32)
    wt_ref[...] = jnp.where(lane == 0, w1, w2)


def _route_call(logits):
    n_tok = logits.shape[0]
    tr = ROUTE_ROWS
    row = lambda i: (i, 0)
    return pl.pallas_call(
        _route_kernel,
        grid=(n_tok // tr,),
        in_specs=[pl.BlockSpec((tr, ROUTER_LANES), row)],
        out_specs=[pl.BlockSpec((tr, ROUTER_LANES), row), pl.BlockSpec((tr, ROUTER_LANES), row),
                   pl.BlockSpec((8, ROUTER_LANES), lambda i: (0, 0))],
        out_shape=[jax.ShapeDtypeStruct((n_tok, ROUTER_LANES), I32),
                   jax.ShapeDtypeStruct((n_tok, ROUTER_LANES), F32),
                   jax.ShapeDtypeStruct((8, ROUTER_LANES), F32)],
        scratch_shapes=[pltpu.VMEM((1, ROUTER_LANES), F32)],
        compiler_params=_cparams("arbitrary"),
        name="route",
    )(logits)


def _layout(idx, cnt):
    n_tok = idx.shape[0]
    rows = MOE_ROWS
    n_blocks = n_tok * TOP_K // rows + N_EXPERTS
    counts = cnt[0, N_GROUPS:N_GROUPS + N_EXPERTS].astype(I32)
    padded = (counts + rows - 1) // rows * rows
    pad_end = jnp.cumsum(padded)
    pad_start = pad_end - padded
    expert = idx[:, 0:TOP_K]
    rank = idx[:, TOP_K:2 * TOP_K]
    start = jnp.sum(jnp.where(expert[:, :, None] == jnp.arange(N_EXPERTS, dtype=I32), pad_start, 0), axis=-1)
    dest = (start + rank).reshape(-1).astype(I32)
    first_row = jnp.arange(n_blocks, dtype=I32) * rows
    block_expert = jnp.minimum(jnp.sum(pad_end[None, :] <= first_row[:, None], axis=1), N_EXPERTS - 1)
    n_used = (pad_end[-1] // rows).reshape(1)
    return dest, block_expert.astype(I32), n_used.astype(I32), pad_end.astype(I32), counts


def _dispatch_kernel(dst_ref, pend_ref, cnt_ref, h_hbm, xs_hbm, zbuf, sem, zsem):
    i = pl.program_id(0)
    n = pl.num_programs(0)
    td = DISPATCH_ROWS
    slot = i % 2

    def zero_copy(e):
        start = pl.multiple_of(pend_ref[e] - MOE_ROWS, MOE_ROWS)
        return pltpu.make_async_copy(zbuf, xs_hbm.at[pl.ds(start, MOE_ROWS), :], zsem)

    @pl.when(i == 0)
    def _():
        zbuf[...] = jnp.zeros_like(zbuf)

        def start(e, c):
            @pl.when(cnt_ref[e] > 0)
            def _():
                zero_copy(e).start()
            return c

        def wait(e, c):
            @pl.when(cnt_ref[e] > 0)
            def _():
                zero_copy(e).wait()
            return c

        lax.fori_loop(0, N_EXPERTS, start, 0)
        lax.fori_loop(0, N_EXPERTS, wait, 0)

        def tail_copy(b):
            return pltpu.make_async_copy(zbuf, xs_hbm.at[pl.ds(pl.multiple_of(b * MOE_ROWS, MOE_ROWS), MOE_ROWS), :],
                                         zsem)

        n_blocks = xs_hbm.shape[0] // MOE_ROWS
        first_free = pend_ref[N_EXPERTS - 1] // MOE_ROWS
        lax.fori_loop(first_free, n_blocks, lambda b, c: (tail_copy(b).start(), c)[1], 0)
        lax.fori_loop(first_free, n_blocks, lambda b, c: (tail_copy(b).wait(), c)[1], 0)

    def body(j, c):
        t = i * td + j
        src = h_hbm.at[pl.ds(t, 1), :]
        for kk in range(TOP_K):
            pltpu.make_async_copy(src, xs_hbm.at[pl.ds(dst_ref[TOP_K * t + kk], 1), :], sem.at[slot]).start()
        return c

    lax.fori_loop(0, td, body, 0)

    def wait_step(s):
        pltpu.make_async_copy(h_hbm.at[pl.ds(0, TOP_K * td), :], xs_hbm.at[pl.ds(0, TOP_K * td), :],
                              sem.at[s]).wait()

    @pl.when(i > 0)
    def _():
        wait_step(1 - slot)

    @pl.when(i == n - 1)
    def _():
        wait_step(slot)


def _dispatch_call(dest, pad_end, counts, h2, n_sorted_rows):
    n_tok, d = h2.shape
    grid_spec = pltpu.PrefetchScalarGridSpec(
        num_scalar_prefetch=3,
        grid=(n_tok // DISPATCH_ROWS,),
        in_specs=[pl.BlockSpec(memory_space=pl.ANY)],
        out_specs=pl.BlockSpec(memory_space=pl.ANY),
        scratch_shapes=[pltpu.VMEM((MOE_ROWS, d), h2.dtype), pltpu.SemaphoreType.DMA((2,)),
                        pltpu.SemaphoreType.DMA(())],
    )
    return pl.pallas_call(
        _dispatch_kernel,
        grid_spec=grid_spec,
        out_shape=jax.ShapeDtypeStruct((n_sorted_rows, d), h2.dtype),
        compiler_params=_cparams("arbitrary"),
        name="moe_dispatch",
    )(dest, pad_end, counts, h2)


def _moe_kernel(be_ref, used_ref, x_ref, wg_ref, wu_ref, wd_ref, y_ref, wgb, wub, wdb):
    b = pl.program_id(0)

    @pl.when(b < used_ref[0])
    def _():
        prev = be_ref[jnp.maximum(b - 1, 0)]

        @pl.when(jnp.logical_or(b == 0, be_ref[b] != prev))
        def _():
            wgb[...] = wg_ref[0].astype(BF16)
            wub[...] = wu_ref[0].astype(BF16)
            wdb[...] = wd_ref[0].astype(BF16)

        xb = x_ref[...].astype(BF16)
        gate = _dot(xb, wgb[...])
        up = _dot(xb, wub[...])
        act = (gate * jax.nn.sigmoid(gate) * up).astype(BF16)
        y_ref[...] = _dot(act, wdb[...])

    @pl.when(b >= used_ref[0])
    def _():
        y_ref[...] = jnp.zeros_like(y_ref)


def _moe_call(x_sorted, block_expert, n_used, w_gate, w_up, w_down):
    n_blocks = block_expert.shape[0]
    rows = MOE_ROWS
    n_exp, d, de = w_gate.shape

    def w_map(b, be, used):
        return (be[jnp.minimum(b, used[0] - 1)], 0, 0)

    grid_spec = pltpu.PrefetchScalarGridSpec(
        num_scalar_prefetch=2,
        grid=(n_blocks,),
        in_specs=[
            pl.BlockSpec((rows, d), lambda b, be, used: (jnp.minimum(b, used[0] - 1), 0)),
            pl.BlockSpec((1, d, de), w_map),
            pl.BlockSpec((1, d, de), w_map),
            pl.BlockSpec((1, de, d), w_map),
        ],
        out_specs=pl.BlockSpec((rows, d), lambda b, be, used: (b, 0)),
        scratch_shapes=[
            pltpu.VMEM((d, de), BF16),
            pltpu.VMEM((d, de), BF16),
            pltpu.VMEM((de, d), BF16),
        ],
    )
    return pl.pallas_call(
        _moe_kernel,
        grid_spec=grid_spec,
        out_shape=jax.ShapeDtypeStruct((n_blocks * rows, d), F32),
        compiler_params=_cparams("arbitrary"),
        name="moe_experts",
    )(block_expert, n_used, x_sorted, w_gate, w_up, w_down)


def _combine_kernel(dst_ref, y_hbm, x_ref, wt_ref, mod_ref, g_ref, o_ref, ybuf, sem, *, final):
    i = pl.program_id(0)
    n = pl.num_programs(0)
    rows = COMBINE_ROWS

    def gather(step, slot):
        def body(j, c):
            for kk in range(TOP_K):
                row = dst_ref[TOP_K * (step * rows + j) + kk]
                pltpu.make_async_copy(y_hbm.at[pl.ds(row, 1), :],
                                      ybuf.at[slot, pl.ds(kk * rows + j, 1), :], sem.at[slot]).start()
            return c
        lax.fori_loop(0, rows, body, 0)

    slot = i % 2

    @pl.when(i == 0)
    def _():
        gather(0, 0)

    @pl.when(i + 1 < n)
    def _():
        gather(i + 1, 1 - slot)

    pltpu.make_async_copy(y_hbm.at[pl.ds(0, TOP_K * rows), :], ybuf.at[slot], sem.at[slot]).wait()
    f = wt_ref[:, 0:1] * ybuf[slot, 0:rows, :]
    for kk in range(1, TOP_K):
        f = f + wt_ref[:, kk:kk + 1] * ybuf[slot, kk * rows:(kk + 1) * rows, :]
    x = x_ref[...] + mod_ref[0, 5:6, :] * f
    if final:
        x = _rms(x, g_ref[...])
    o_ref[...] = x


def _combine_call(dest, y_sorted, x_mid, wts, mod, g_final, *, n_rows, n_lat, seq, final):
    d = x_mid.shape[1]
    rows = COMBINE_ROWS
    n_batch = n_lat // seq
    grid_spec = pltpu.PrefetchScalarGridSpec(
        num_scalar_prefetch=1,
        grid=(n_rows // rows,),
        in_specs=[
            pl.BlockSpec(memory_space=pl.ANY),
            pl.BlockSpec((rows, d), lambda i, dst: (i, 0)),
            pl.BlockSpec((rows, ROUTER_LANES), lambda i, dst: (i, 0)),
            pl.BlockSpec((1, N_MOD, d), lambda i, dst: (jnp.minimum(i * rows // seq, n_batch), 0, 0)),
            pl.BlockSpec((1, d), lambda i, dst: (0, 0)),
        ],
        out_specs=pl.BlockSpec((rows, d), lambda i, dst: (i, 0)),
        scratch_shapes=[pltpu.VMEM((2, TOP_K * rows, d), F32), pltpu.SemaphoreType.DMA((2,))],
    )
    return pl.pallas_call(
        functools.partial(_combine_kernel, final=final),
        grid_spec=grid_spec,
        out_shape=jax.ShapeDtypeStruct((n_rows, d), F32),
        compiler_params=_cparams("arbitrary"),
        name="moe_combine",
    )(dest, y_sorted, x_mid, wts, mod, g_final)


def _rope_tables(seq, pad_rows):
    rows = seq // GRID_W
    row = jnp.broadcast_to(jnp.arange(rows, dtype=I32)[:, None], (rows, GRID_W)).reshape(-1)
    col = jnp.broadcast_to(jnp.arange(GRID_W, dtype=I32)[None, :], (rows, GRID_W)).reshape(-1)
    axis_dim = HEAD_DIM // 2
    inv_freq = ROPE_THETA ** (-jnp.arange(0, axis_dim, 2, dtype=F32) / axis_dim)
    ang_r = row.astype(F32)[:, None] * inv_freq
    ang_c = col.astype(F32)[:, None] * inv_freq
    ang = jnp.concatenate([ang_r, ang_r, ang_c, ang_c], axis=-1)
    cos, sin = jnp.cos(ang), jnp.sin(ang)
    lo = (jnp.arange(HEAD_DIM) % (HEAD_DIM // 2)) < (HEAD_DIM // 4)
    sin_lo = jnp.where(lo[None, :], -sin, 0.0)
    sin_hi = jnp.where(lo[None, :], 0.0, sin)
    cos = jnp.concatenate([cos, jnp.ones((pad_rows, HEAD_DIM), F32)], axis=0)
    zeros = jnp.zeros((pad_rows, HEAD_DIM), F32)
    return cos, jnp.concatenate([sin_lo, zeros], axis=0), jnp.concatenate([sin_hi, zeros], axis=0)


def kernel(x, c, ctx, c_ctx, w_ada, b_ada, norm1_g, w_in, q_norm_g, k_norm_g, sconv_w, conf_dw_w,
           conf_dw_b, conf_ln_g, conf_ln_b, grp_norm_g, w_out, norm2_g, router_g_w, router_g_b,
           router_e_w, router_e_b, exp_w_gate, exp_w_up, exp_w_down, final_g):
    n_batch, seq, d = x.shape
    ctx_len = ctx.shape[1]
    depth = w_ada.shape[0]
    n_lat = n_batch * seq
    n_ctx = n_batch * ctx_len
    n_all = n_lat + n_ctx
    d_sconv = sconv_w.shape[2]
    assert n_batch + 1 <= MOD_ROWS and seq % TM_PROJ == 0 and n_ctx % TM_PROJ == 0
    assert seq % CONV_ROWS == 0 and ctx_len % CONV_ROWS == 0 and seq % GRID_W == 0
    assert n_lat % DISPATCH_ROWS == 0 and n_ctx % DISPATCH_ROWS == 0 and n_lat % ctx_len == 0
    assert N_GROUPS + N_EXPERTS <= ROUTER_LANES and seq % TQ_ATTN == 0 and ctx_len % TQ_ATTN == 0

    c_all = jnp.concatenate([c, c_ctx[None, :], jnp.zeros((MOD_ROWS - n_batch - 1, d), F32)], axis=0)
    mod_all = _ada_call(c_all, w_ada, b_ada).reshape(depth, MOD_ROWS, N_MOD, d)
    cos, sin_lo, sin_hi = _rope_tables(seq, TM_PROJ)
    row2 = lambda a: a.reshape(1, -1)
    lat_tiles = n_lat // TM_PROJ
    xa, xb, b_off = x.reshape(n_lat, d), ctx.reshape(n_ctx, d), 0

    for i in range(depth):
        last = i == depth - 1
        mod = mod_all[i]
        n_rows = n_lat if last else n_all
        q, k, v, gb, cu, glu = _inproj_call(
            xa, xb, b_off, mod, row2(norm1_g[i]), w_in[i].astype(BF16), row2(q_norm_g[i]),
            row2(k_norm_g[i]), cos, sin_lo, sin_hi, n_all=n_all, n_lat=n_lat, seq=seq)
        g_attn = row2(grp_norm_g[i][:D_ATTN])
        ya = _attn_call(q, k, v, g_attn, n_lat=n_lat, seq=seq, ctx_len=ctx_len, ctx_queries=not last)
        ys, yc = _conv_call(
            gb, cu, glu, sconv_w[i], conf_dw_w[i], row2(conf_dw_b[i]), row2(conf_ln_g[i]),
            row2(conf_ln_b[i]), row2(grp_norm_g[i][D_ATTN:D_ATTN + d_sconv]),
            row2(grp_norm_g[i][D_ATTN + d_sconv:]), n_rows=n_rows, n_lat=n_lat, seq=seq, ctx_len=ctx_len)
        n_pad = ROUTER_LANES - N_GROUPS - N_EXPERTS
        w_router = jnp.concatenate([router_g_w[i], router_e_w[i], jnp.zeros((d, n_pad), F32)],
                                   axis=1).astype(BF16)
        b_router = jnp.concatenate([router_g_b[i], router_e_b[i], jnp.zeros((n_pad,), F32)])[None, :]
        x_mid, h2, logits = _outproj_call(
            ya, ys, yc, xa, xb, b_off, mod, w_out[i].astype(BF16), row2(norm2_g[i]), w_router, b_router,
            n_rows=n_rows, n_lat=n_lat, seq=seq)
        idx, wts, cnt = _route_call(logits)
        dest, block_expert, n_used, pad_end, counts = _layout(idx, cnt)
        x_sorted = _dispatch_call(dest, pad_end, counts, h2, block_expert.shape[0] * MOE_ROWS)
        y_sorted = _moe_call(x_sorted, block_expert, n_used, exp_w_gate[i], exp_w_up[i], exp_w_down[i])
        x_all = _combine_call(dest, y_sorted, x_mid, wts, mod, row2(final_g),
                              n_rows=n_rows, n_lat=n_lat, seq=seq, final=last)
        xa, xb, b_off = x_all, x_all, lat_tiles
    return x_all.reshape(n_batch, seq, d)
```

```python
import functools

import jax
import jax.numpy as jnp
from jax import lax
from jax.experimental import pallas as pl
from jax.experimental.pallas import tpu as pltpu

F32 = jnp.float32
BF16 = jnp.bfloat16
I32 = jnp.int32

HEAD_DIM = 128
N_HEADS = 8
N_KV_HEADS = 2
GQA_GROUP = N_HEADS // N_KV_HEADS
D_ATTN = N_HEADS * HEAD_DIM
D_KV = N_KV_HEADS * HEAD_DIM
GRID_W = 64
ROPE_THETA = 10000.0
SCONV_WIDTH = 3
CONF_WIDTH = 31
N_GROUPS = 4
EXPERTS_PER_GROUP = 8
N_EXPERTS = N_GROUPS * EXPERTS_PER_GROUP
TOP_K = 2
N_MOD = 6
EPS = 1e-6
LOG2_E = 1.4426950408889634
DMA_UNROLL = 8

MOD_ROWS = 32
ROUTER_LANES = 128
TM_PROJ = 512
TQ_ATTN = 256
CONV_ROWS = 256
CONV_HALO = 16
ROUTE_ROWS = 512
DISPATCH_ROWS = 512
MOE_ROWS = 512
COMBINE_ROWS = 256
ADA_TN = 1024
VMEM_LIMIT = 56 * 1024 * 1024


def _cparams(*sem):
    return pltpu.CompilerParams(dimension_semantics=sem, vmem_limit_bytes=VMEM_LIMIT)


def _dot(a, b):
    return jnp.dot(a, b, preferred_element_type=F32)


def _rms(x, g):
    return x * lax.rsqrt(jnp.mean(x * x, axis=-1, keepdims=True) + EPS) * g


def _ada_kernel(c_ref, w_ref, b_ref, o_ref):
    a = c_ref[...]
    a = a * jax.nn.sigmoid(a)
    o_ref[0] = _dot(a.astype(BF16), w_ref[0].astype(BF16)) + b_ref[0]


def _ada_call(c_all, w_ada, b_ada):
    depth, d, n = w_ada.shape
    return pl.pallas_call(
        _ada_kernel,
        grid=(depth, n // ADA_TN),
        in_specs=[
            pl.BlockSpec((MOD_ROWS, d), lambda l, j: (0, 0)),
            pl.BlockSpec((1, d, ADA_TN), lambda l, j: (l, 0, j)),
            pl.BlockSpec((1, 1, ADA_TN), lambda l, j: (l, 0, j)),
        ],
        out_specs=pl.BlockSpec((1, MOD_ROWS, ADA_TN), lambda l, j: (l, 0, j)),
        out_shape=jax.ShapeDtypeStruct((depth, MOD_ROWS, n), F32),
        compiler_params=_cparams("arbitrary", "arbitrary"),
        name="ada_mod",
    )(c_all, w_ada, b_ada.reshape(depth, 1, n))


def _rope(t, cos, sin_lo, sin_hi):
    return t * cos + pltpu.roll(t, 96, 1) * sin_lo + pltpu.roll(t, 32, 1) * sin_hi


def _inproj_kernel(xa_ref, xb_ref, mod_ref, g1_ref, w_ref, qg_ref, kg_ref, cos_ref, slo_ref, shi_ref,
                   q_ref, k_ref, v_ref, gb_ref, cu_ref, glu_ref, *, a_tiles):
    x = jnp.where(pl.program_id(0) < a_tiles, xa_ref[...], xb_ref[...])
    shift = mod_ref[0, 0:1, :]
    scale = mod_ref[0, 1:2, :]
    h = _rms(x, g1_ref[...]) * (1.0 + scale) + shift
    hb = h.astype(BF16)
    cos, slo, shi = cos_ref[...], slo_ref[...], shi_ref[...]
    qg, kg = qg_ref[...], kg_ref[...]
    qscale = HEAD_DIM ** -0.5 * LOG2_E
    half = D_ATTN // 2
    for j in range(2):
        qc = _dot(hb, w_ref[:, j * half:(j + 1) * half])
        for hh in range(half // HEAD_DIM):
            t = _rms(qc[:, hh * HEAD_DIM:(hh + 1) * HEAD_DIM], qg)
            t = _rope(t, cos, slo, shi) * qscale
            q_ref[:, j * half + hh * HEAD_DIM: j * half + (hh + 1) * HEAD_DIM] = t.astype(BF16)
    kv = _dot(hb, w_ref[:, D_ATTN:D_ATTN + 2 * D_KV])
    for hh in range(N_KV_HEADS):
        t = _rms(kv[:, hh * HEAD_DIM:(hh + 1) * HEAD_DIM], kg)
        k_ref[:, hh * HEAD_DIM:(hh + 1) * HEAD_DIM] = _rope(t, cos, slo, shi).astype(BF16)
    v_ref[...] = kv[:, D_KV:].astype(BF16)
    c0 = D_ATTN + 2 * D_KV
    dg = gb_ref.shape[1]
    gb_ref[...] = _dot(hb, w_ref[:, c0:c0 + dg])
    cu_ref[...] = _dot(hb, w_ref[:, c0 + dg:c0 + 2 * dg]) * _dot(hb, w_ref[:, c0 + 2 * dg:c0 + 3 * dg])
    ca = _dot(hb, w_ref[:, c0 + 3 * dg:c0 + 4 * dg])
    cg = _dot(hb, w_ref[:, c0 + 4 * dg:c0 + 5 * dg])
    glu_ref[...] = ca * jax.nn.sigmoid(cg)


def _two_source_specs(tm, d, a_tiles, b_off):
    return [pl.BlockSpec((tm, d), lambda i, *_: (jnp.minimum(i, a_tiles - 1), 0)),
            pl.BlockSpec((tm, d), lambda i, *_: (jnp.maximum(i - a_tiles, 0) + b_off, 0))]


def _inproj_call(xa, xb, b_off, mod, g1, w_in_bf, qg, kg, cos, slo, shi, *, n_all, n_lat, seq):
    d = xa.shape[1]
    n_in = w_in_bf.shape[1]
    tm = TM_PROJ
    dg = (n_in - D_ATTN - 2 * D_KV) // 5
    lat_tiles = n_lat // tm
    seq_tiles = seq // tm
    n_batch = n_lat // seq

    def mod_map(i):
        return (jnp.minimum(i * tm // seq, n_batch), 0, 0)

    def rope_map(i):
        return (jnp.where(i < lat_tiles, i % seq_tiles, seq_tiles), 0)

    row = lambda i: (i, 0)
    fix = lambda i: (0, 0)
    return pl.pallas_call(
        functools.partial(_inproj_kernel, a_tiles=lat_tiles),
        grid=(n_all // tm,),
        in_specs=_two_source_specs(tm, d, lat_tiles, b_off) + [
            pl.BlockSpec((1, N_MOD, d), mod_map),
            pl.BlockSpec((1, d), fix),
            pl.BlockSpec((d, n_in), fix, pipeline_mode=pl.Buffered(1)),
            pl.BlockSpec((1, HEAD_DIM), fix),
            pl.BlockSpec((1, HEAD_DIM), fix),
            pl.BlockSpec((tm, HEAD_DIM), rope_map),
            pl.BlockSpec((tm, HEAD_DIM), rope_map),
            pl.BlockSpec((tm, HEAD_DIM), rope_map),
        ],
        out_specs=[
            pl.BlockSpec((tm, D_ATTN), row),
            pl.BlockSpec((tm, D_KV), row),
            pl.BlockSpec((tm, D_KV), row),
            pl.BlockSpec((tm, dg), row),
            pl.BlockSpec((tm, dg), row),
            pl.BlockSpec((tm, dg), row),
        ],
        out_shape=[
            jax.ShapeDtypeStruct((n_all, D_ATTN), BF16),
            jax.ShapeDtypeStruct((n_all, D_KV), BF16),
            jax.ShapeDtypeStruct((n_all, D_KV), BF16),
            jax.ShapeDtypeStruct((n_all, dg), F32),
            jax.ShapeDtypeStruct((n_all, dg), F32),
            jax.ShapeDtypeStruct((n_all, dg), F32),
        ],
        compiler_params=_cparams("arbitrary"),
        name="in_proj",
    )(xa, xb, mod, g1, w_in_bf, qg, kg, cos, slo, shi)


def _attend(q_ref, kv_refs, g_ref, o_ref):
    outs = []
    for hh in range(N_HEADS):
        kvh = hh // GQA_GROUP
        q = q_ref[:, hh * HEAD_DIM:(hh + 1) * HEAD_DIM]
        ks = [k_ref[:, kvh * HEAD_DIM:(kvh + 1) * HEAD_DIM] for k_ref, _ in kv_refs]
        vs = [v_ref[:, kvh * HEAD_DIM:(kvh + 1) * HEAD_DIM] for _, v_ref in kv_refs]
        ss = [lax.dot_general(q, k, (((1,), (1,)), ((), ())), preferred_element_type=F32) for k in ks]
        m = ss[0].max(axis=-1, keepdims=True)
        for s in ss[1:]:
            m = jnp.maximum(m, s.max(axis=-1, keepdims=True))
        ps = [jnp.exp2(s - m) for s in ss]
        den = ps[0].sum(axis=-1, keepdims=True)
        for p in ps[1:]:
            den = den + p.sum(axis=-1, keepdims=True)
        o = _dot(ps[0].astype(BF16), vs[0])
        for p, v in zip(ps[1:], vs[1:]):
            o = o + _dot(p.astype(BF16), v)
        outs.append(o / den)
    y = jnp.concatenate(outs, axis=-1)
    o_ref[...] = _rms(y, g_ref[...]).astype(BF16)


def _attn_kernel(q_ref, kl_ref, vl_ref, kc_ref, vc_ref, g_ref, o_ref, *, lat_steps, ctx_steps):
    if ctx_steps == 0:
        _attend(q_ref, [(kl_ref, vl_ref), (kc_ref, vc_ref)], g_ref, o_ref)
        return
    i = pl.program_id(1)

    @pl.when(i < lat_steps)
    def _():
        _attend(q_ref, [(kl_ref, vl_ref), (kc_ref, vc_ref)], g_ref, o_ref)

    @pl.when(i >= lat_steps)
    def _():
        _attend(q_ref, [(kc_ref, vc_ref)], g_ref, o_ref)


def _attn_call(q, k, v, g_attn, *, n_lat, seq, ctx_len, ctx_queries):
    tq = TQ_ATTN
    n_batch = n_lat // seq
    qt = seq // tq
    ct = ctx_len // tq if ctx_queries else 0
    n_out = n_lat + (n_batch * ctx_len if ctx_queries else 0)

    def q_map(b, i):
        return (jnp.where(i < qt, b * qt + i, n_lat // tq + b * ct + (i - qt)), 0)

    lat_kv = pl.BlockSpec((seq, D_KV), lambda b, i: (b, 0))
    ctx_kv = pl.BlockSpec((ctx_len, D_KV), lambda b, i: (n_lat // ctx_len + b, 0))
    return pl.pallas_call(
        functools.partial(_attn_kernel, lat_steps=qt, ctx_steps=ct),
        grid=(n_batch, qt + ct),
        in_specs=[pl.BlockSpec((tq, D_ATTN), q_map), lat_kv, lat_kv, ctx_kv, ctx_kv,
                  pl.BlockSpec((1, D_ATTN), lambda b, i: (0, 0))],
        out_specs=pl.BlockSpec((tq, D_ATTN), q_map),
        out_shape=jax.ShapeDtypeStruct((n_out, D_ATTN), BF16),
        compiler_params=_cparams("arbitrary", "arbitrary"),
        name="attention",
    )(q, k, v, k, v, g_attn)


def _conv_kernel(gb_ref, cu_ref, cu_p_ref, cu_n_ref, gl_ref, gl_p_ref, gl_n_ref,
                 sw_ref, dw_ref, db_ref, lg_ref, lb_ref, gs_ref, gc_ref,
                 ys_ref, yc_ref, win_s, win_c, *, lat_chunks, seq_chunks, ctx_chunks):
    i = pl.program_id(0)
    r = CONV_ROWS
    hl = CONV_HALO
    pos = jnp.where(i < lat_chunks, i % seq_chunks, (i - lat_chunks) % ctx_chunks)
    last = jnp.where(i < lat_chunks, seq_chunks - 1, ctx_chunks - 1)
    keep_p = (pos > 0).astype(F32)
    keep_n = (pos < last).astype(F32)

    win_s[0:hl, :] = cu_p_ref[...] * keep_p
    win_s[hl:hl + r, :] = cu_ref[...]
    win_s[hl + r:, :] = cu_n_ref[...] * keep_n
    win_c[0:hl, :] = gl_p_ref[...] * keep_p
    win_c[hl:hl + r, :] = gl_ref[...]
    win_c[hl + r:, :] = gl_n_ref[...] * keep_n

    pad_s = (SCONV_WIDTH - 1) // 2
    acc = sw_ref[0:1, :] * win_s[hl - pad_s:hl - pad_s + r, :]
    for t in range(1, SCONV_WIDTH):
        acc = acc + sw_ref[t:t + 1, :] * win_s[hl - pad_s + t:hl - pad_s + t + r, :]
    ys_ref[...] = _rms(gb_ref[...] * acc, gs_ref[...]).astype(BF16)

    pad_c = (CONF_WIDTH - 1) // 2
    acc = dw_ref[0:1, :] * win_c[hl - pad_c:hl - pad_c + r, :]
    for t in range(1, CONF_WIDTH):
        acc = acc + dw_ref[t:t + 1, :] * win_c[hl - pad_c + t:hl - pad_c + t + r, :]
    u = acc + db_ref[...]
    mu = jnp.mean(u, axis=-1, keepdims=True)
    uc = u - mu
    var = jnp.mean(uc * uc, axis=-1, keepdims=True)
    z = uc * lax.rsqrt(var + EPS) * lg_ref[...] + lb_ref[...]
    z = z * jax.nn.sigmoid(z)
    yc_ref[...] = _rms(z, gc_ref[...]).astype(BF16)


def _conv_call(gb, cu, glu, sconv_w, dw_w, dw_b, ln_g, ln_b, g_s, g_c, *, n_rows, n_lat, seq, ctx_len):
    dg = gb.shape[1]
    r, hl = CONV_ROWS, CONV_HALO
    n_chunks = n_rows // r
    per = r // hl
    n_halo = n_rows // hl
    cur = lambda i: (i, 0)
    prev = lambda i: (jnp.maximum(i * per - 1, 0), 0)
    nxt = lambda i: (jnp.minimum((i + 1) * per, n_halo - 1), 0)
    fix = lambda i: (0, 0)
    kern = functools.partial(_conv_kernel, lat_chunks=n_lat // r, seq_chunks=seq // r,
                             ctx_chunks=max(ctx_len // r, 1))
    return pl.pallas_call(
        kern,
        grid=(n_chunks,),
        in_specs=[
            pl.BlockSpec((r, dg), cur),
            pl.BlockSpec((r, dg), cur), pl.BlockSpec((hl, dg), prev), pl.BlockSpec((hl, dg), nxt),
            pl.BlockSpec((r, dg), cur), pl.BlockSpec((hl, dg), prev), pl.BlockSpec((hl, dg), nxt),
            pl.BlockSpec((SCONV_WIDTH, dg), fix),
            pl.BlockSpec((CONF_WIDTH, dg), fix),
            pl.BlockSpec((1, dg), fix), pl.BlockSpec((1, dg), fix), pl.BlockSpec((1, dg), fix),
            pl.BlockSpec((1, dg), fix), pl.BlockSpec((1, dg), fix),
        ],
        out_specs=[pl.BlockSpec((r, dg), cur), pl.BlockSpec((r, dg), cur)],
        out_shape=[jax.ShapeDtypeStruct((n_rows, dg), BF16), jax.ShapeDtypeStruct((n_rows, dg), BF16)],
        scratch_shapes=[pltpu.VMEM((r + 2 * hl, dg), F32), pltpu.VMEM((r + 2 * hl, dg), F32)],
        compiler_params=_cparams("arbitrary"),
        name="group_convs",
    )(gb, cu, cu, cu, glu, glu, glu, sconv_w, dw_w, dw_b, ln_g, ln_b, g_s, g_c)


def _outproj_kernel(ya_ref, ys_ref, yc_ref, xa_ref, xb_ref, mod_ref, wo_ref, g2_ref, wr_ref, br_ref,
                    xo_ref, h2_ref, lg_ref, *, a_tiles):
    da = ya_ref.shape[1]
    ds_ = ys_ref.shape[1]
    mix = _dot(ya_ref[...], wo_ref[0:da, :])
    mix = mix + _dot(ys_ref[...], wo_ref[da:da + ds_, :])
    mix = mix + _dot(yc_ref[...], wo_ref[da + ds_:, :])
    x = jnp.where(pl.program_id(0) < a_tiles, xa_ref[...], xb_ref[...])
    x = x + mod_ref[0, 2:3, :] * mix
    xo_ref[...] = x
    h2 = _rms(x, g2_ref[...]) * (1.0 + mod_ref[0, 4:5, :]) + mod_ref[0, 3:4, :]
    h2_ref[...] = h2
    lg_ref[...] = _dot(h2.astype(BF16), wr_ref[...]) + br_ref[...]


def _outproj_call(ya, ys, yc, xa, xb, b_off, mod, w_out_bf, g2, w_router, b_router, *, n_rows, n_lat, seq):
    d = xa.shape[1]
    tm = TM_PROJ
    n_batch = n_lat // seq
    row = lambda i: (i, 0)
    fix = lambda i: (0, 0)
    mod_map = lambda i: (jnp.minimum(i * tm // seq, n_batch), 0, 0)
    return pl.pallas_call(
        functools.partial(_outproj_kernel, a_tiles=n_lat // tm),
        grid=(n_rows // tm,),
        in_specs=[
            pl.BlockSpec((tm, ya.shape[1]), row),
            pl.BlockSpec((tm, ys.shape[1]), row),
            pl.BlockSpec((tm, yc.shape[1]), row),
        ] + _two_source_specs(tm, d, n_lat // tm, b_off) + [
            pl.BlockSpec((1, N_MOD, d), mod_map),
            pl.BlockSpec(w_out_bf.shape, fix, pipeline_mode=pl.Buffered(1)),
            pl.BlockSpec((1, d), fix),
            pl.BlockSpec((d, ROUTER_LANES), fix),
            pl.BlockSpec((1, ROUTER_LANES), fix),
        ],
        out_specs=[pl.BlockSpec((tm, d), row), pl.BlockSpec((tm, d), row),
                   pl.BlockSpec((tm, ROUTER_LANES), row)],
        out_shape=[jax.ShapeDtypeStruct((n_rows, d), F32), jax.ShapeDtypeStruct((n_rows, d), F32),
                   jax.ShapeDtypeStruct((n_rows, ROUTER_LANES), F32)],
        compiler_params=_cparams("arbitrary"),
        name="out_proj",
    )(ya, ys, yc, xa, xb, mod, w_out_bf, g2, w_router, b_router)


def _first_lane_of_max(vals, lane_f):
    m = vals.max(axis=-1, keepdims=True)
    return m, jnp.where(vals == m, lane_f, float(ROUTER_LANES)).min(axis=-1, keepdims=True)


def _route_kernel(lg_ref, idx_ref, wt_ref, cnt_ref, carry):
    i = pl.program_id(0)
    tr = lg_ref.shape[0]

    @pl.when(i == 0)
    def _():
        carry[...] = jnp.zeros_like(carry)

    lg = lg_ref[...]
    lane = lax.broadcasted_iota(I32, lg.shape, 1)
    lane_f = lane.astype(F32)
    neg = -jnp.inf
    gmask = lane < N_GROUPS
    gl = jnp.where(gmask, lg, neg)
    gmax, g_idx = _first_lane_of_max(gl, lane_f)
    g_top = 1.0 / jnp.where(gmask, jnp.exp(gl - gmax), 0.0).sum(axis=-1, keepdims=True)
    lo = float(N_GROUPS) + g_idx * float(EXPERTS_PER_GROUP)
    emask = jnp.logical_and(lane_f >= lo, lane_f < lo + float(EXPERTS_PER_GROUP))
    el = jnp.where(emask, lg, neg)
    m1, l1 = _first_lane_of_max(el, lane_f)
    o1 = lane_f == l1
    el2 = jnp.where(o1, neg, el)
    m2, l2 = _first_lane_of_max(el2, lane_f)
    o2 = lane_f == l2
    ratio = jnp.exp(m2 - m1)
    w1 = g_top / (1.0 + ratio)
    w2 = w1 * ratio

    onehot = jnp.where(jnp.logical_or(o1, o2), 1.0, 0.0)
    rows_i = lax.broadcasted_iota(I32, (tr, tr), 0)
    cols_i = lax.broadcasted_iota(I32, (tr, tr), 1)
    below = jnp.where(cols_i < rows_i, 1.0, 0.0).astype(BF16)
    base = carry[...] + _dot(below, onehot.astype(BF16))
    r1 = jnp.where(o1, base, 0.0).sum(axis=-1, keepdims=True)
    r2 = jnp.where(o2, base, 0.0).sum(axis=-1, keepdims=True)
    carry[...] = carry[...] + onehot.sum(axis=0, keepdims=True)
    cnt_ref[...] = jnp.broadcast_to(carry[...], cnt_ref.shape)

    e1 = l1 - float(N_GROUPS)
    e2 = l2 - float(N_GROUPS)
    idx = jnp.where(lane == 0, e1, jnp.where(lane == 1, e2, jnp.where(lane == 2, r1, r2)))
    idx_ref[...] = idx.astype(I32)
    wt_ref[...] = jnp.where(lane == 0, w1, w2)


def _route_call(logits):
    n_tok = logits.shape[0]
    tr = ROUTE_ROWS
    row = lambda i: (i, 0)
    return pl.pallas_call(
        _route_kernel,
        grid=(n_tok // tr,),
        in_specs=[pl.BlockSpec((tr, ROUTER_LANES), row)],
        out_specs=[pl.BlockSpec((tr, ROUTER_LANES), row), pl.BlockSpec((tr, ROUTER_LANES), row),
                   pl.BlockSpec((8, ROUTER_LANES), lambda i: (0, 0))],
        out_shape=[jax.ShapeDtypeStruct((n_tok, ROUTER_LANES), I32),
                   jax.ShapeDtypeStruct((n_tok, ROUTER_LANES), F32),
                   jax.ShapeDtypeStruct((8, ROUTER_LANES), F32)],
        scratch_shapes=[pltpu.VMEM((1, ROUTER_LANES), F32)],
        compiler_params=_cparams("arbitrary"),
        name="route",
    )(logits)


def _layout(idx, cnt):
    n_tok = idx.shape[0]
    rows = MOE_ROWS
    n_blocks = n_tok * TOP_K // rows + N_EXPERTS
    counts = cnt[0, N_GROUPS:N_GROUPS + N_EXPERTS].astype(I32)
    padded = (counts + rows - 1) // rows * rows
    pad_end = jnp.cumsum(padded)
    pad_start = pad_end - padded
    expert = idx[:, 0:TOP_K]
    rank = idx[:, TOP_K:2 * TOP_K]
    start = jnp.sum(jnp.where(expert[:, :, None] == jnp.arange(N_EXPERTS, dtype=I32), pad_start, 0), axis=-1)
    dest = (start + rank).reshape(-1).astype(I32)
    first_row = jnp.arange(n_blocks, dtype=I32) * rows
    block_expert = jnp.minimum(jnp.sum(pad_end[None, :] <= first_row[:, None], axis=1), N_EXPERTS - 1)
    n_used = (pad_end[-1] // rows).reshape(1)
    return dest, block_expert.astype(I32), n_used.astype(I32), pad_end.astype(I32), counts


def _dispatch_kernel(dst_ref, pend_ref, cnt_ref, h_ref, xs_hbm, zbuf, sem, zsem):
    i = pl.program_id(0)
    td = h_ref.shape[0]

    def zero_copy(e):
        start = pl.multiple_of(pend_ref[e] - MOE_ROWS, MOE_ROWS)
        return pltpu.make_async_copy(zbuf, xs_hbm.at[pl.ds(start, MOE_ROWS), :], zsem)

    @pl.when(i == 0)
    def _():
        zbuf[...] = jnp.zeros_like(zbuf)

        def start(e, c):
            @pl.when(cnt_ref[e] > 0)
            def _():
                zero_copy(e).start()
            return c

        def wait(e, c):
            @pl.when(cnt_ref[e] > 0)
            def _():
                zero_copy(e).wait()
            return c

        lax.fori_loop(0, N_EXPERTS, start, 0)
        lax.fori_loop(0, N_EXPERTS, wait, 0)

        def tail_copy(b):
            return pltpu.make_async_copy(zbuf, xs_hbm.at[pl.ds(pl.multiple_of(b * MOE_ROWS, MOE_ROWS), MOE_ROWS), :],
                                         zsem)

        n_blocks = xs_hbm.shape[0] // MOE_ROWS
        first_free = pend_ref[N_EXPERTS - 1] // MOE_ROWS
        lax.fori_loop(first_free, n_blocks, lambda b, c: (tail_copy(b).start(), c)[1], 0)
        lax.fori_loop(first_free, n_blocks, lambda b, c: (tail_copy(b).wait(), c)[1], 0)

    def group(g, c):
        base = pl.multiple_of(g * DMA_UNROLL, DMA_UNROLL)
        for u in range(DMA_UNROLL):
            src = h_ref.at[pl.ds(base + u, 1), :]
            for kk in range(TOP_K):
                row = dst_ref[TOP_K * (i * td + base + u) + kk]
                pltpu.make_async_copy(src, xs_hbm.at[pl.ds(row, 1), :], sem).start()
        return c

    lax.fori_loop(0, td // DMA_UNROLL, group, 0)
    for kk in range(TOP_K):
        pltpu.make_async_copy(h_ref, xs_hbm.at[pl.ds(0, td), :], sem).wait()


def _dispatch_call(dest, pad_end, counts, h2, n_sorted_rows):
    n_tok, d = h2.shape
    td = DISPATCH_ROWS
    grid_spec = pltpu.PrefetchScalarGridSpec(
        num_scalar_prefetch=3,
        grid=(n_tok // td,),
        in_specs=[pl.BlockSpec((td, d), lambda i, dst, pend, cnt: (i, 0))],
        out_specs=pl.BlockSpec(memory_space=pl.ANY),
        scratch_shapes=[pltpu.VMEM((MOE_ROWS, d), h2.dtype), pltpu.SemaphoreType.DMA(()),
                        pltpu.SemaphoreType.DMA(())],
    )
    return pl.pallas_call(
        _dispatch_kernel,
        grid_spec=grid_spec,
        out_shape=jax.ShapeDtypeStruct((n_sorted_rows, d), h2.dtype),
        compiler_params=_cparams("arbitrary"),
        name="moe_dispatch",
    )(dest, pad_end, counts, h2)


def _moe_kernel(be_ref, used_ref, x_ref, wg_ref, wu_ref, wd_ref, y_ref, wgb, wub, wdb):
    b = pl.program_id(0)

    @pl.when(b < used_ref[0])
    def _():
        prev = be_ref[jnp.maximum(b - 1, 0)]

        @pl.when(jnp.logical_or(b == 0, be_ref[b] != prev))
        def _():
            wgb[...] = wg_ref[0, 0].astype(BF16)
            wub[...] = wu_ref[0, 0].astype(BF16)
            wdb[...] = wd_ref[0, 0].astype(BF16)

        xb = x_ref[...].astype(BF16)
        gate = _dot(xb, wgb[...])
        up = _dot(xb, wub[...])
        act = (gate * jax.nn.sigmoid(gate) * up).astype(BF16)
        y_ref[...] = _dot(act, wdb[...])

    @pl.when(b >= used_ref[0])
    def _():
        y_ref[...] = jnp.zeros_like(y_ref)


def _moe_call(x_sorted, block_expert, n_used, w_gate, w_up, w_down, layer):
    n_blocks = block_expert.shape[0]
    rows = MOE_ROWS
    _, n_exp, d, de = w_gate.shape

    def w_map(b, be, used):
        return (layer, be[jnp.minimum(b, used[0] - 1)], 0, 0)

    grid_spec = pltpu.PrefetchScalarGridSpec(
        num_scalar_prefetch=2,
        grid=(n_blocks,),
        in_specs=[
            pl.BlockSpec((rows, d), lambda b, be, used: (jnp.minimum(b, used[0] - 1), 0)),
            pl.BlockSpec((1, 1, d, de), w_map),
            pl.BlockSpec((1, 1, d, de), w_map),
            pl.BlockSpec((1, 1, de, d), w_map),
        ],
        out_specs=pl.BlockSpec((rows, d), lambda b, be, used: (b, 0)),
        scratch_shapes=[
            pltpu.VMEM((d, de), BF16),
            pltpu.VMEM((d, de), BF16),
            pltpu.VMEM((de, d), BF16),
        ],
    )
    return pl.pallas_call(
        _moe_kernel,
        grid_spec=grid_spec,
        out_shape=jax.ShapeDtypeStruct((n_blocks * rows, d), F32),
        compiler_params=_cparams("arbitrary"),
        name="moe_experts",
    )(block_expert, n_used, x_sorted, w_gate, w_up, w_down)


def _combine_kernel(dst_ref, y_hbm, x_ref, wt_ref, mod_ref, g_ref, o_ref, ybuf, sem, *, final):
    i = pl.program_id(0)
    n = pl.num_programs(0)
    rows = COMBINE_ROWS

    def gather(step, slot):
        def group(g, c):
            base = pl.multiple_of(g * DMA_UNROLL, DMA_UNROLL)
            for u in range(DMA_UNROLL):
                for kk in range(TOP_K):
                    row = dst_ref[TOP_K * (step * rows + base + u) + kk]
                    pltpu.make_async_copy(y_hbm.at[pl.ds(row, 1), :],
                                          ybuf.at[slot, pl.ds(kk * rows + base + u, 1), :], sem.at[slot]).start()
            return c
        lax.fori_loop(0, rows // DMA_UNROLL, group, 0)

    slot = i % 2

    @pl.when(i == 0)
    def _():
        gather(0, 0)

    @pl.when(i + 1 < n)
    def _():
        gather(i + 1, 1 - slot)

    pltpu.make_async_copy(y_hbm.at[pl.ds(0, TOP_K * rows), :], ybuf.at[slot], sem.at[slot]).wait()
    f = wt_ref[:, 0:1] * ybuf[slot, 0:rows, :]
    for kk in range(1, TOP_K):
        f = f + wt_ref[:, kk:kk + 1] * ybuf[slot, kk * rows:(kk + 1) * rows, :]
    x = x_ref[...] + mod_ref[0, 5:6, :] * f
    if final:
        x = _rms(x, g_ref[...])
    o_ref[...] = x


def _combine_call(dest, y_sorted, x_mid, wts, mod, g_final, *, n_rows, n_lat, seq, final):
    d = x_mid.shape[1]
    rows = COMBINE_ROWS
    n_batch = n_lat // seq
    grid_spec = pltpu.PrefetchScalarGridSpec(
        num_scalar_prefetch=1,
        grid=(n_rows // rows,),
        in_specs=[
            pl.BlockSpec(memory_space=pl.ANY),
            pl.BlockSpec((rows, d), lambda i, dst: (i, 0)),
            pl.BlockSpec((rows, ROUTER_LANES), lambda i, dst: (i, 0)),
            pl.BlockSpec((1, N_MOD, d), lambda i, dst: (jnp.minimum(i * rows // seq, n_batch), 0, 0)),
            pl.BlockSpec((1, d), lambda i, dst: (0, 0)),
        ],
        out_specs=pl.BlockSpec((rows, d), lambda i, dst: (i, 0)),
        scratch_shapes=[pltpu.VMEM((2, TOP_K * rows, d), F32), pltpu.SemaphoreType.DMA((2,))],
    )
    return pl.pallas_call(
        functools.partial(_combine_kernel, final=final),
        grid_spec=grid_spec,
        out_shape=jax.ShapeDtypeStruct((n_rows, d), F32),
        compiler_params=_cparams("arbitrary"),
        name="moe_combine",
    )(dest, y_sorted, x_mid, wts, mod, g_final)


def _rope_tables(seq, pad_rows):
    rows = seq // GRID_W
    row = jnp.broadcast_to(jnp.arange(rows, dtype=I32)[:, None], (rows, GRID_W)).reshape(-1)
    col = jnp.broadcast_to(jnp.arange(GRID_W, dtype=I32)[None, :], (rows, GRID_W)).reshape(-1)
    axis_dim = HEAD_DIM // 2
    inv_freq = ROPE_THETA ** (-jnp.arange(0, axis_dim, 2, dtype=F32) / axis_dim)
    ang_r = row.astype(F32)[:, None] * inv_freq
    ang_c = col.astype(F32)[:, None] * inv_freq
    ang = jnp.concatenate([ang_r, ang_r, ang_c, ang_c], axis=-1)
    cos, sin = jnp.cos(ang), jnp.sin(ang)
    lo = (jnp.arange(HEAD_DIM) % (HEAD_DIM // 2)) < (HEAD_DIM // 4)
    sin_lo = jnp.where(lo[None, :], -sin, 0.0)
    sin_hi = jnp.where(lo[None, :], 0.0, sin)
    cos = jnp.concatenate([cos, jnp.ones((pad_rows, HEAD_DIM), F32)], axis=0)
    zeros = jnp.zeros((pad_rows, HEAD_DIM), F32)
    return cos, jnp.concatenate([sin_lo, zeros], axis=0), jnp.concatenate([sin_hi, zeros], axis=0)


def kernel(x, c, ctx, c_ctx, w_ada, b_ada, norm1_g, w_in, q_norm_g, k_norm_g, sconv_w, conf_dw_w,
           conf_dw_b, conf_ln_g, conf_ln_b, grp_norm_g, w_out, norm2_g, router_g_w, router_g_b,
           router_e_w, router_e_b, exp_w_gate, exp_w_up, exp_w_down, final_g):
    n_batch, seq, d = x.shape
    ctx_len = ctx.shape[1]
    depth = w_ada.shape[0]
    n_lat = n_batch * seq
    n_ctx = n_batch * ctx_len
    n_all = n_lat + n_ctx
    d_sconv = sconv_w.shape[2]
    assert n_batch + 1 <= MOD_ROWS and seq % TM_PROJ == 0 and n_ctx % TM_PROJ == 0
    assert seq % CONV_ROWS == 0 and ctx_len % CONV_ROWS == 0 and seq % GRID_W == 0
    assert n_lat % DISPATCH_ROWS == 0 and n_ctx % DISPATCH_ROWS == 0 and n_lat % ctx_len == 0
    assert N_GROUPS + N_EXPERTS <= ROUTER_LANES and seq % TQ_ATTN == 0 and ctx_len % TQ_ATTN == 0

    c_all = jnp.concatenate([c, c_ctx[None, :], jnp.zeros((MOD_ROWS - n_batch - 1, d), F32)], axis=0)
    mod_all = _ada_call(c_all, w_ada, b_ada).reshape(depth, MOD_ROWS, N_MOD, d)
    cos, sin_lo, sin_hi = _rope_tables(seq, TM_PROJ)
    row2 = lambda a: a.reshape(1, -1)
    lat_tiles = n_lat // TM_PROJ
    xa, xb, b_off = x.reshape(n_lat, d), ctx.reshape(n_ctx, d), 0

    for i in range(depth):
        last = i == depth - 1
        mod = mod_all[i]
        n_rows = n_lat if last else n_all
        q, k, v, gb, cu, glu = _inproj_call(
            xa, xb, b_off, mod, row2(norm1_g[i]), w_in[i].astype(BF16), row2(q_norm_g[i]),
            row2(k_norm_g[i]), cos, sin_lo, sin_hi, n_all=n_all, n_lat=n_lat, seq=seq)
        g_attn = row2(grp_norm_g[i][:D_ATTN])
        ya = _attn_call(q, k, v, g_attn, n_lat=n_lat, seq=seq, ctx_len=ctx_len, ctx_queries=not last)
        ys, yc = _conv_call(
            gb, cu, glu, sconv_w[i], conf_dw_w[i], row2(conf_dw_b[i]), row2(conf_ln_g[i]),
            row2(conf_ln_b[i]), row2(grp_norm_g[i][D_ATTN:D_ATTN + d_sconv]),
            row2(grp_norm_g[i][D_ATTN + d_sconv:]), n_rows=n_rows, n_lat=n_lat, seq=seq, ctx_len=ctx_len)
        n_pad = ROUTER_LANES - N_GROUPS - N_EXPERTS
        w_router = jnp.concatenate([router_g_w[i], router_e_w[i], jnp.zeros((d, n_pad), F32)],
                                   axis=1).astype(BF16)
        b_router = jnp.concatenate([router_g_b[i], router_e_b[i], jnp.zeros((n_pad,), F32)])[None, :]
        x_mid, h2, logits = _outproj_call(
            ya, ys, yc, xa, xb, b_off, mod, w_out[i].astype(BF16), row2(norm2_g[i]), w_router, b_router,
            n_rows=n_rows, n_lat=n_lat, seq=seq)
        idx, wts, cnt = _route_call(logits)
        dest, block_expert, n_used, pad_end, counts = _layout(idx, cnt)
        x_sorted = _dispatch_call(dest, pad_end, counts, h2, block_expert.shape[0] * MOE_ROWS)
        y_sorted = _moe_call(x_sorted, block_expert, n_used, exp_w_gate, exp_w_up, exp_w_down, i)
        x_all = _combine_call(dest, y_sorted, x_mid, wts, mod, row2(final_g),
                              n_rows=n_rows, n_lat=n_lat, seq=seq, final=last)
        xa, xb, b_off = x_all, x_all, lat_tiles
    return x_all.reshape(n_batch, seq, d)
```

```python
import functools

import jax
import jax.numpy as jnp
from jax import lax
from jax.experimental import pallas as pl
from jax.experimental.pallas import tpu as pltpu

F32 = jnp.float32
BF16 = jnp.bfloat16
I32 = jnp.int32

HEAD_DIM = 128
N_HEADS = 8
N_KV_HEADS = 2
GQA_GROUP = N_HEADS // N_KV_HEADS
D_ATTN = N_HEADS * HEAD_DIM
D_KV = N_KV_HEADS * HEAD_DIM
GRID_W = 64
ROPE_THETA = 10000.0
SCONV_WIDTH = 3
CONF_WIDTH = 31
N_GROUPS = 4
EXPERTS_PER_GROUP = 8
N_EXPERTS = N_GROUPS * EXPERTS_PER_GROUP
TOP_K = 2
N_MOD = 6
EPS = 1e-6
LOG2_E = 1.4426950408889634
SUBLANES = 8
DMA_UNROLL = 8

MOD_ROWS = 32
ROUTER_LANES = 128
TM_PROJ = 512
TQ_ATTN = 256
CONV_ROWS = 256
CONV_HALO = 16
ROUTE_ROWS = 512
DISPATCH_ROWS = 512
MOE_ROWS = 512
COMBINE_ROWS = 256
ADA_TN = 1024
VMEM_LIMIT = 56 * 1024 * 1024


def _cparams(*sem):
    return pltpu.CompilerParams(dimension_semantics=sem, vmem_limit_bytes=VMEM_LIMIT)


def _dot(a, b):
    return jnp.dot(a, b, preferred_element_type=F32)


def _rms(x, g):
    return x * lax.rsqrt(jnp.mean(x * x, axis=-1, keepdims=True) + EPS) * g


def _ada_kernel(c_ref, w_ref, b_ref, o_ref):
    a = c_ref[...]
    a = a * jax.nn.sigmoid(a)
    o_ref[0] = _dot(a.astype(BF16), w_ref[0].astype(BF16)) + b_ref[0]


def _ada_call(c_all, w_ada, b_ada):
    depth, d, n = w_ada.shape
    return pl.pallas_call(
        _ada_kernel,
        grid=(depth, n // ADA_TN),
        in_specs=[
            pl.BlockSpec((MOD_ROWS, d), lambda l, j: (0, 0)),
            pl.BlockSpec((1, d, ADA_TN), lambda l, j: (l, 0, j)),
            pl.BlockSpec((1, 1, ADA_TN), lambda l, j: (l, 0, j)),
        ],
        out_specs=pl.BlockSpec((1, MOD_ROWS, ADA_TN), lambda l, j: (l, 0, j)),
        out_shape=jax.ShapeDtypeStruct((depth, MOD_ROWS, n), F32),
        compiler_params=_cparams("arbitrary", "arbitrary"),
        name="ada_mod",
    )(c_all, w_ada, b_ada.reshape(depth, 1, n))


def _rope(t, cos, sin_lo, sin_hi):
    return t * cos + pltpu.roll(t, 96, 1) * sin_lo + pltpu.roll(t, 32, 1) * sin_hi


def _inproj_kernel(xa_ref, xb_ref, mod_ref, g1_ref, w_ref, qg_ref, kg_ref, cos_ref, slo_ref, shi_ref,
                   q_ref, k_ref, v_ref, gb_ref, cu_ref, glu_ref, *, a_tiles):
    x = jnp.where(pl.program_id(0) < a_tiles, xa_ref[...], xb_ref[...])
    shift = mod_ref[0, 0:1, :]
    scale = mod_ref[0, 1:2, :]
    h = _rms(x, g1_ref[...]) * (1.0 + scale) + shift
    hb = h.astype(BF16)
    cos, slo, shi = cos_ref[...], slo_ref[...], shi_ref[...]
    qg, kg = qg_ref[...], kg_ref[...]
    qscale = HEAD_DIM ** -0.5 * LOG2_E
    half = D_ATTN // 2
    for j in range(2):
        qc = _dot(hb, w_ref[:, j * half:(j + 1) * half])
        for hh in range(half // HEAD_DIM):
            t = _rms(qc[:, hh * HEAD_DIM:(hh + 1) * HEAD_DIM], qg)
            t = _rope(t, cos, slo, shi) * qscale
            q_ref[:, j * half + hh * HEAD_DIM: j * half + (hh + 1) * HEAD_DIM] = t.astype(BF16)
    kv = _dot(hb, w_ref[:, D_ATTN:D_ATTN + 2 * D_KV])
    for hh in range(N_KV_HEADS):
        t = _rms(kv[:, hh * HEAD_DIM:(hh + 1) * HEAD_DIM], kg)
        k_ref[hh * HEAD_DIM:(hh + 1) * HEAD_DIM, :] = _rope(t, cos, slo, shi).T.astype(BF16)
    v_ref[...] = kv[:, D_KV:].astype(BF16)
    c0 = D_ATTN + 2 * D_KV
    dg = gb_ref.shape[1]
    gb_ref[...] = _dot(hb, w_ref[:, c0:c0 + dg])
    cu_ref[...] = _dot(hb, w_ref[:, c0 + dg:c0 + 2 * dg]) * _dot(hb, w_ref[:, c0 + 2 * dg:c0 + 3 * dg])
    ca = _dot(hb, w_ref[:, c0 + 3 * dg:c0 + 4 * dg])
    cg = _dot(hb, w_ref[:, c0 + 4 * dg:c0 + 5 * dg])
    glu_ref[...] = ca * jax.nn.sigmoid(cg)


def _two_source_specs(tm, d, a_tiles, b_off):
    return [pl.BlockSpec((tm, d), lambda i, *_: (jnp.minimum(i, a_tiles - 1), 0)),
            pl.BlockSpec((tm, d), lambda i, *_: (jnp.maximum(i - a_tiles, 0) + b_off, 0))]


def _inproj_call(xa, xb, b_off, mod, g1, w_in_bf, qg, kg, cos, slo, shi, *, n_all, n_lat, seq):
    d = xa.shape[1]
    n_in = w_in_bf.shape[1]
    tm = TM_PROJ
    dg = (n_in - D_ATTN - 2 * D_KV) // 5
    lat_tiles = n_lat // tm
    seq_tiles = seq // tm
    n_batch = n_lat // seq

    def mod_map(i):
        return (jnp.minimum(i * tm // seq, n_batch), 0, 0)

    def rope_map(i):
        return (jnp.where(i < lat_tiles, i % seq_tiles, seq_tiles), 0)

    row = lambda i: (i, 0)
    fix = lambda i: (0, 0)
    return pl.pallas_call(
        functools.partial(_inproj_kernel, a_tiles=lat_tiles),
        grid=(n_all // tm,),
        in_specs=_two_source_specs(tm, d, lat_tiles, b_off) + [
            pl.BlockSpec((1, N_MOD, d), mod_map),
            pl.BlockSpec((1, d), fix),
            pl.BlockSpec((d, n_in), fix, pipeline_mode=pl.Buffered(1)),
            pl.BlockSpec((1, HEAD_DIM), fix),
            pl.BlockSpec((1, HEAD_DIM), fix),
            pl.BlockSpec((tm, HEAD_DIM), rope_map),
            pl.BlockSpec((tm, HEAD_DIM), rope_map),
            pl.BlockSpec((tm, HEAD_DIM), rope_map),
        ],
        out_specs=[
            pl.BlockSpec((tm, D_ATTN), row),
            pl.BlockSpec((D_KV, tm), lambda i: (0, i)),
            pl.BlockSpec((tm, D_KV), row),
            pl.BlockSpec((tm, dg), row),
            pl.BlockSpec((tm, dg), row),
            pl.BlockSpec((tm, dg), row),
        ],
        out_shape=[
            jax.ShapeDtypeStruct((n_all, D_ATTN), BF16),
            jax.ShapeDtypeStruct((D_KV, n_all), BF16),
            jax.ShapeDtypeStruct((n_all, D_KV), BF16),
            jax.ShapeDtypeStruct((n_all, dg), F32),
            jax.ShapeDtypeStruct((n_all, dg), F32),
            jax.ShapeDtypeStruct((n_all, dg), F32),
        ],
        compiler_params=_cparams("arbitrary"),
        name="in_proj",
    )(xa, xb, mod, g1, w_in_bf, qg, kg, cos, slo, shi)


def _attend(q_ref, kv_refs, g_ref, o_ref):
    outs = []
    for hh in range(N_HEADS):
        kvh = hh // GQA_GROUP
        q = q_ref[:, hh * HEAD_DIM:(hh + 1) * HEAD_DIM]
        kts = [kt_ref[kvh * HEAD_DIM:(kvh + 1) * HEAD_DIM, :] for kt_ref, _ in kv_refs]
        vs = [v_ref[:, kvh * HEAD_DIM:(kvh + 1) * HEAD_DIM] for _, v_ref in kv_refs]
        ss = [_dot(q, kt) for kt in kts]
        m = ss[0].max(axis=-1, keepdims=True)
        for s in ss[1:]:
            m = jnp.maximum(m, s.max(axis=-1, keepdims=True))
        ps = [jnp.exp2(s - m) for s in ss]
        den = ps[0].sum(axis=-1, keepdims=True)
        for p in ps[1:]:
            den = den + p.sum(axis=-1, keepdims=True)
        o = _dot(ps[0].astype(BF16), vs[0])
        for p, v in zip(ps[1:], vs[1:]):
            o = o + _dot(p.astype(BF16), v)
        outs.append(o / den)
    y = jnp.concatenate(outs, axis=-1)
    o_ref[...] = _rms(y, g_ref[...]).astype(BF16)


def _attn_kernel(q_ref, kl_ref, vl_ref, kc_ref, vc_ref, g_ref, o_ref, *, lat_steps, ctx_steps):
    if ctx_steps == 0:
        _attend(q_ref, [(kl_ref, vl_ref), (kc_ref, vc_ref)], g_ref, o_ref)
        return
    i = pl.program_id(1)

    @pl.when(i < lat_steps)
    def _():
        _attend(q_ref, [(kl_ref, vl_ref), (kc_ref, vc_ref)], g_ref, o_ref)

    @pl.when(i >= lat_steps)
    def _():
        _attend(q_ref, [(kc_ref, vc_ref)], g_ref, o_ref)


def _attn_call(q, k, v, g_attn, *, n_lat, seq, ctx_len, ctx_queries):
    tq = TQ_ATTN
    n_batch = n_lat // seq
    qt = seq // tq
    ct = ctx_len // tq if ctx_queries else 0
    n_out = n_lat + (n_batch * ctx_len if ctx_queries else 0)

    def q_map(b, i):
        return (jnp.where(i < qt, b * qt + i, n_lat // tq + b * ct + (i - qt)), 0)

    lat_v = pl.BlockSpec((seq, D_KV), lambda b, i: (b, 0))
    ctx_v = pl.BlockSpec((ctx_len, D_KV), lambda b, i: (n_lat // ctx_len + b, 0))
    lat_kt = pl.BlockSpec((D_KV, seq), lambda b, i: (0, b))
    ctx_kt = pl.BlockSpec((D_KV, ctx_len), lambda b, i: (0, n_lat // ctx_len + b))
    return pl.pallas_call(
        functools.partial(_attn_kernel, lat_steps=qt, ctx_steps=ct),
        grid=(n_batch, qt + ct),
        in_specs=[pl.BlockSpec((tq, D_ATTN), q_map), lat_kt, lat_v, ctx_kt, ctx_v,
                  pl.BlockSpec((1, D_ATTN), lambda b, i: (0, 0))],
        out_specs=pl.BlockSpec((tq, D_ATTN), q_map),
        out_shape=jax.ShapeDtypeStruct((n_out, D_ATTN), BF16),
        compiler_params=_cparams("arbitrary", "arbitrary"),
        name="attention",
    )(q, k, v, k, v, g_attn)


def _conv_taps(win, shifted, w_ref, first, width, rows):
    span = win.shape[0] - SUBLANES
    acc = None
    for j in range(SUBLANES):
        taps = [t for t in range(width) if (first + t) % SUBLANES == j]
        if not taps:
            continue
        if j == 0:
            src = win
        else:
            shifted[j - 1, :, :] = win[j:j + span, :]
            src = shifted.at[j - 1]
        for t in taps:
            a = first + t - j
            term = w_ref[t:t + 1, :] * src[a:a + rows, :]
            acc = term if acc is None else acc + term
    return acc


def _conv_kernel(gb_ref, cu_ref, cu_p_ref, cu_n_ref, gl_ref, gl_p_ref, gl_n_ref,
                 sw_ref, dw_ref, db_ref, lg_ref, lb_ref, gs_ref, gc_ref,
                 ys_ref, yc_ref, win_s, win_c, shifted, *, lat_chunks, seq_chunks, ctx_chunks):
    i = pl.program_id(0)
    r = CONV_ROWS
    hl = CONV_HALO
    pos = jnp.where(i < lat_chunks, i % seq_chunks, (i - lat_chunks) % ctx_chunks)
    last = jnp.where(i < lat_chunks, seq_chunks - 1, ctx_chunks - 1)
    keep_p = (pos > 0).astype(F32)
    keep_n = (pos < last).astype(F32)

    win_s[0:hl, :] = cu_p_ref[...] * keep_p
    win_s[hl:hl + r, :] = cu_ref[...]
    win_s[hl + r:, :] = cu_n_ref[...] * keep_n
    win_c[0:hl, :] = gl_p_ref[...] * keep_p
    win_c[hl:hl + r, :] = gl_ref[...]
    win_c[hl + r:, :] = gl_n_ref[...] * keep_n

    acc = _conv_taps(win_s, shifted, sw_ref, hl - (SCONV_WIDTH - 1) // 2, SCONV_WIDTH, r)
    ys_ref[...] = _rms(gb_ref[...] * acc, gs_ref[...]).astype(BF16)

    acc = _conv_taps(win_c, shifted, dw_ref, hl - (CONF_WIDTH - 1) // 2, CONF_WIDTH, r)
    u = acc + db_ref[...]
    mu = jnp.mean(u, axis=-1, keepdims=True)
    uc = u - mu
    var = jnp.mean(uc * uc, axis=-1, keepdims=True)
    z = uc * lax.rsqrt(var + EPS) * lg_ref[...] + lb_ref[...]
    z = z * jax.nn.sigmoid(z)
    yc_ref[...] = _rms(z, gc_ref[...]).astype(BF16)


def _conv_call(gb, cu, glu, sconv_w, dw_w, dw_b, ln_g, ln_b, g_s, g_c, *, n_rows, n_lat, seq, ctx_len):
    dg = gb.shape[1]
    r, hl = CONV_ROWS, CONV_HALO
    n_chunks = n_rows // r
    per = r // hl
    n_halo = n_rows // hl
    cur = lambda i: (i, 0)
    prev = lambda i: (jnp.maximum(i * per - 1, 0), 0)
    nxt = lambda i: (jnp.minimum((i + 1) * per, n_halo - 1), 0)
    fix = lambda i: (0, 0)
    kern = functools.partial(_conv_kernel, lat_chunks=n_lat // r, seq_chunks=seq // r,
                             ctx_chunks=max(ctx_len // r, 1))
    return pl.pallas_call(
        kern,
        grid=(n_chunks,),
        in_specs=[
            pl.BlockSpec((r, dg), cur),
            pl.BlockSpec((r, dg), cur), pl.BlockSpec((hl, dg), prev), pl.BlockSpec((hl, dg), nxt),
            pl.BlockSpec((r, dg), cur), pl.BlockSpec((hl, dg), prev), pl.BlockSpec((hl, dg), nxt),
            pl.BlockSpec((SCONV_WIDTH, dg), fix),
            pl.BlockSpec((CONF_WIDTH, dg), fix),
            pl.BlockSpec((1, dg), fix), pl.BlockSpec((1, dg), fix), pl.BlockSpec((1, dg), fix),
            pl.BlockSpec((1, dg), fix), pl.BlockSpec((1, dg), fix),
        ],
        out_specs=[pl.BlockSpec((r, dg), cur), pl.BlockSpec((r, dg), cur)],
        out_shape=[jax.ShapeDtypeStruct((n_rows, dg), BF16), jax.ShapeDtypeStruct((n_rows, dg), BF16)],
        scratch_shapes=[pltpu.VMEM((r + 2 * hl, dg), F32), pltpu.VMEM((r + 2 * hl, dg), F32),
                        pltpu.VMEM((SUBLANES - 1, r + 2 * hl - SUBLANES, dg), F32)],
        compiler_params=_cparams("arbitrary"),
        name="group_convs",
    )(gb, cu, cu, cu, glu, glu, glu, sconv_w, dw_w, dw_b, ln_g, ln_b, g_s, g_c)


def _outproj_kernel(ya_ref, ys_ref, yc_ref, xa_ref, xb_ref, mod_ref, wo_ref, g2_ref, wr_ref, br_ref,
                    xo_ref, h2_ref, lg_ref, *, a_tiles):
    da = ya_ref.shape[1]
    ds_ = ys_ref.shape[1]
    mix = _dot(ya_ref[...], wo_ref[0:da, :])
    mix = mix + _dot(ys_ref[...], wo_ref[da:da + ds_, :])
    mix = mix + _dot(yc_ref[...], wo_ref[da + ds_:, :])
    x = jnp.where(pl.program_id(0) < a_tiles, xa_ref[...], xb_ref[...])
    x = x + mod_ref[0, 2:3, :] * mix
    xo_ref[...] = x
    h2 = _rms(x, g2_ref[...]) * (1.0 + mod_ref[0, 4:5, :]) + mod_ref[0, 3:4, :]
    h2_ref[...] = h2
    lg_ref[...] = _dot(h2.astype(BF16), wr_ref[...]) + br_ref[...]


def _outproj_call(ya, ys, yc, xa, xb, b_off, mod, w_out_bf, g2, w_router, b_router, *, n_rows, n_lat, seq):
    d = xa.shape[1]
    tm = TM_PROJ
    n_batch = n_lat // seq
    row = lambda i: (i, 0)
    fix = lambda i: (0, 0)
    mod_map = lambda i: (jnp.minimum(i * tm // seq, n_batch), 0, 0)
    return pl.pallas_call(
        functools.partial(_outproj_kernel, a_tiles=n_lat // tm),
        grid=(n_rows // tm,),
        in_specs=[
            pl.BlockSpec((tm, ya.shape[1]), row),
            pl.BlockSpec((tm, ys.shape[1]), row),
            pl.BlockSpec((tm, yc.shape[1]), row),
        ] + _two_source_specs(tm, d, n_lat // tm, b_off) + [
            pl.BlockSpec((1, N_MOD, d), mod_map),
            pl.BlockSpec(w_out_bf.shape, fix, pipeline_mode=pl.Buffered(1)),
            pl.BlockSpec((1, d), fix),
            pl.BlockSpec((d, ROUTER_LANES), fix),
            pl.BlockSpec((1, ROUTER_LANES), fix),
        ],
        out_specs=[pl.BlockSpec((tm, d), row), pl.BlockSpec((tm, d), row),
                   pl.BlockSpec((tm, ROUTER_LANES), row)],
        out_shape=[jax.ShapeDtypeStruct((n_rows, d), F32), jax.ShapeDtypeStruct((n_rows, d), F32),
                   jax.ShapeDtypeStruct((n_rows, ROUTER_LANES), F32)],
        compiler_params=_cparams("arbitrary"),
        name="out_proj",
    )(ya, ys, yc, xa, xb, mod, w_out_bf, g2, w_router, b_router)


def _first_lane_of_max(vals, lane_f):
    m = vals.max(axis=-1, keepdims=True)
    return m, jnp.where(vals == m, lane_f, float(ROUTER_LANES)).min(axis=-1, keepdims=True)


def _route_kernel(lg_ref, idx_ref, wt_ref, cnt_ref, carry):
    i = pl.program_id(0)
    tr = lg_ref.shape[0]

    @pl.when(i == 0)
    def _():
        carry[...] = jnp.zeros_like(carry)

    lg = lg_ref[...]
    lane = lax.broadcasted_iota(I32, lg.shape, 1)
    lane_f = lane.astype(F32)
    neg = -jnp.inf
    gmask = lane < N_GROUPS
    gl = jnp.where(gmask, lg, neg)
    gmax, g_idx = _first_lane_of_max(gl, lane_f)
    g_top = 1.0 / jnp.where(gmask, jnp.exp(gl - gmax), 0.0).sum(axis=-1, keepdims=True)
    lo = float(N_GROUPS) + g_idx * float(EXPERTS_PER_GROUP)
    emask = jnp.logical_and(lane_f >= lo, lane_f < lo + float(EXPERTS_PER_GROUP))
    el = jnp.where(emask, lg, neg)
    m1, l1 = _first_lane_of_max(el, lane_f)
    o1 = lane_f == l1
    el2 = jnp.where(o1, neg, el)
    m2, l2 = _first_lane_of_max(el2, lane_f)
    o2 = lane_f == l2
    ratio = jnp.exp(m2 - m1)
    w1 = g_top / (1.0 + ratio)
    w2 = w1 * ratio

    onehot = jnp.where(jnp.logical_or(o1, o2), 1.0, 0.0)
    rows_i = lax.broadcasted_iota(I32, (tr, tr), 0)
    cols_i = lax.broadcasted_iota(I32, (tr, tr), 1)
    below = jnp.where(cols_i < rows_i, 1.0, 0.0).astype(BF16)
    base = carry[...] + _dot(below, onehot.astype(BF16))
    r1 = jnp.where(o1, base, 0.0).sum(axis=-1, keepdims=True)
    r2 = jnp.where(o2, base, 0.0).sum(axis=-1, keepdims=True)
    carry[...] = carry[...] + onehot.sum(axis=0, keepdims=True)
    cnt_ref[...] = jnp.broadcast_to(carry[...], cnt_ref.shape)

    e1 = l1 - float(N_GROUPS)
    e2 = l2 - float(N_GROUPS)
    idx = jnp.where(lane == 0, e1, jnp.where(lane == 1, e2, jnp.where(lane == 2, r1, r2)))
    idx_ref[...] = idx.astype(I32)
    wt_ref[...] = jnp.where(lane == 0, w1, w2)


def _route_call(logits):
    n_tok = logits.shape[0]
    tr = ROUTE_ROWS
    row = lambda i: (i, 0)
    return pl.pallas_call(
        _route_kernel,
        grid=(n_tok // tr,),
        in_specs=[pl.BlockSpec((tr, ROUTER_LANES), row)],
        out_specs=[pl.BlockSpec((tr, ROUTER_LANES), row), pl.BlockSpec((tr, ROUTER_LANES), row),
                   pl.BlockSpec((8, ROUTER_LANES), lambda i: (0, 0))],
        out_shape=[jax.ShapeDtypeStruct((n_tok, ROUTER_LANES), I32),
                   jax.ShapeDtypeStruct((n_tok, ROUTER_LANES), F32),
                   jax.ShapeDtypeStruct((8, ROUTER_LANES), F32)],
        scratch_shapes=[pltpu.VMEM((1, ROUTER_LANES), F32)],
        compiler_params=_cparams("arbitrary"),
        name="route",
    )(logits)


def _layout(idx, cnt):
    n_tok = idx.shape[0]
    rows = MOE_ROWS
    n_blocks = n_tok * TOP_K // rows + N_EXPERTS
    counts = cnt[0, N_GROUPS:N_GROUPS + N_EXPERTS].astype(I32)
    padded = (counts + rows - 1) // rows * rows
    pad_end = jnp.cumsum(padded)
    pad_start = pad_end - padded
    expert = idx[:, 0:TOP_K]
    rank = idx[:, TOP_K:2 * TOP_K]
    start = jnp.sum(jnp.where(expert[:, :, None] == jnp.arange(N_EXPERTS, dtype=I32), pad_start, 0), axis=-1)
    dest = (start + rank).reshape(-1).astype(I32)
    first_row = jnp.arange(n_blocks, dtype=I32) * rows
    block_expert = jnp.minimum(jnp.sum(pad_end[None, :] <= first_row[:, None], axis=1), N_EXPERTS - 1)
    n_used = (pad_end[-1] // rows).reshape(1)
    return dest, block_expert.astype(I32), n_used.astype(I32), pad_end.astype(I32), counts


def _dispatch_kernel(dst_ref, pend_ref, cnt_ref, h_ref, xs_hbm, zbuf, sem, zsem):
    i = pl.program_id(0)
    td = h_ref.shape[0]

    def zero_copy(e):
        start = pl.multiple_of(pend_ref[e] - MOE_ROWS, MOE_ROWS)
        return pltpu.make_async_copy(zbuf, xs_hbm.at[pl.ds(start, MOE_ROWS), :], zsem)

    @pl.when(i == 0)
    def _():
        zbuf[...] = jnp.zeros_like(zbuf)

        def start(e, c):
            @pl.when(cnt_ref[e] > 0)
            def _():
                zero_copy(e).start()
            return c

        def wait(e, c):
            @pl.when(cnt_ref[e] > 0)
            def _():
                zero_copy(e).wait()
            return c

        lax.fori_loop(0, N_EXPERTS, start, 0)
        lax.fori_loop(0, N_EXPERTS, wait, 0)

        def tail_copy(b):
            return pltpu.make_async_copy(zbuf, xs_hbm.at[pl.ds(pl.multiple_of(b * MOE_ROWS, MOE_ROWS), MOE_ROWS), :],
                                         zsem)

        n_blocks = xs_hbm.shape[0] // MOE_ROWS
        first_free = pend_ref[N_EXPERTS - 1] // MOE_ROWS
        lax.fori_loop(first_free, n_blocks, lambda b, c: (tail_copy(b).start(), c)[1], 0)
        lax.fori_loop(first_free, n_blocks, lambda b, c: (tail_copy(b).wait(), c)[1], 0)

    def group(g, c):
        base = pl.multiple_of(g * DMA_UNROLL, DMA_UNROLL)
        for u in range(DMA_UNROLL):
            src = h_ref.at[pl.ds(base + u, 1), :]
            for kk in range(TOP_K):
                row = dst_ref[TOP_K * (i * td + base + u) + kk]
                pltpu.make_async_copy(src, xs_hbm.at[pl.ds(row, 1), :], sem).start()
        return c

    lax.fori_loop(0, td // DMA_UNROLL, group, 0)
    for kk in range(TOP_K):
        pltpu.make_async_copy(h_ref, xs_hbm.at[pl.ds(0, td), :], sem).wait()


def _dispatch_call(dest, pad_end, counts, h2, n_sorted_rows):
    n_tok, d = h2.shape
    td = DISPATCH_ROWS
    grid_spec = pltpu.PrefetchScalarGridSpec(
        num_scalar_prefetch=3,
        grid=(n_tok // td,),
        in_specs=[pl.BlockSpec((td, d), lambda i, dst, pend, cnt: (i, 0))],
        out_specs=pl.BlockSpec(memory_space=pl.ANY),
        scratch_shapes=[pltpu.VMEM((MOE_ROWS, d), h2.dtype), pltpu.SemaphoreType.DMA(()),
                        pltpu.SemaphoreType.DMA(())],
    )
    return pl.pallas_call(
        _dispatch_kernel,
        grid_spec=grid_spec,
        out_shape=jax.ShapeDtypeStruct((n_sorted_rows, d), h2.dtype),
        compiler_params=_cparams("arbitrary"),
        name="moe_dispatch",
    )(dest, pad_end, counts, h2)


def _moe_kernel(be_ref, used_ref, x_ref, wg_ref, wu_ref, wd_ref, y_ref, wgb, wub, wdb):
    b = pl.program_id(0)

    @pl.when(b < used_ref[0])
    def _():
        prev = be_ref[jnp.maximum(b - 1, 0)]

        @pl.when(jnp.logical_or(b == 0, be_ref[b] != prev))
        def _():
            wgb[...] = wg_ref[0, 0].astype(BF16)
            wub[...] = wu_ref[0, 0].astype(BF16)
            wdb[...] = wd_ref[0, 0].astype(BF16)

        xb = x_ref[...].astype(BF16)
        gate = _dot(xb, wgb[...])
        up = _dot(xb, wub[...])
        act = (gate * jax.nn.sigmoid(gate) * up).astype(BF16)
        y_ref[...] = _dot(act, wdb[...])

    @pl.when(b >= used_ref[0])
    def _():
        y_ref[...] = jnp.zeros_like(y_ref)


def _moe_call(x_sorted, block_expert, n_used, w_gate, w_up, w_down, layer):
    n_blocks = block_expert.shape[0]
    rows = MOE_ROWS
    _, n_exp, d, de = w_gate.shape

    def w_map(b, be, used):
        return (layer, be[jnp.minimum(b, used[0] - 1)], 0, 0)

    grid_spec = pltpu.PrefetchScalarGridSpec(
        num_scalar_prefetch=2,
        grid=(n_blocks,),
        in_specs=[
            pl.BlockSpec((rows, d), lambda b, be, used: (jnp.minimum(b, used[0] - 1), 0)),
            pl.BlockSpec((1, 1, d, de), w_map),
            pl.BlockSpec((1, 1, d, de), w_map),
            pl.BlockSpec((1, 1, de, d), w_map),
        ],
        out_specs=pl.BlockSpec((rows, d), lambda b, be, used: (b, 0)),
        scratch_shapes=[
            pltpu.VMEM((d, de), BF16),
            pltpu.VMEM((d, de), BF16),
            pltpu.VMEM((de, d), BF16),
        ],
    )
    return pl.pallas_call(
        _moe_kernel,
        grid_spec=grid_spec,
        out_shape=jax.ShapeDtypeStruct((n_blocks * rows, d), F32),
        compiler_params=_cparams("arbitrary"),
        name="moe_experts",
    )(block_expert, n_used, x_sorted, w_gate, w_up, w_down)


def _combine_kernel(dst_ref, y_hbm, x_ref, wt_ref, mod_ref, g_ref, o_ref, ybuf, sem, *, final):
    i = pl.program_id(0)
    n = pl.num_programs(0)
    rows = COMBINE_ROWS

    def gather(step, slot):
        def group(g, c):
            base = pl.multiple_of(g * DMA_UNROLL, DMA_UNROLL)
            for u in range(DMA_UNROLL):
                for kk in range(TOP_K):
                    row = dst_ref[TOP_K * (step * rows + base + u) + kk]
                    pltpu.make_async_copy(y_hbm.at[pl.ds(row, 1), :],
                                          ybuf.at[slot, pl.ds(kk * rows + base + u, 1), :], sem.at[slot]).start()
            return c
        lax.fori_loop(0, rows // DMA_UNROLL, group, 0)

    slot = i % 2

    @pl.when(i == 0)
    def _():
        gather(0, 0)

    @pl.when(i + 1 < n)
    def _():
        gather(i + 1, 1 - slot)

    pltpu.make_async_copy(y_hbm.at[pl.ds(0, TOP_K * rows), :], ybuf.at[slot], sem.at[slot]).wait()
    f = wt_ref[:, 0:1] * ybuf[slot, 0:rows, :]
    for kk in range(1, TOP_K):
        f = f + wt_ref[:, kk:kk + 1] * ybuf[slot, kk * rows:(kk + 1) * rows, :]
    x = x_ref[...] + mod_ref[0, 5:6, :] * f
    if final:
        x = _rms(x, g_ref[...])
    o_ref[...] = x


def _combine_call(dest, y_sorted, x_mid, wts, mod, g_final, *, n_rows, n_lat, seq, final):
    d = x_mid.shape[1]
    rows = COMBINE_ROWS
    n_batch = n_lat // seq
    grid_spec = pltpu.PrefetchScalarGridSpec(
        num_scalar_prefetch=1,
        grid=(n_rows // rows,),
        in_specs=[
            pl.BlockSpec(memory_space=pl.ANY),
            pl.BlockSpec((rows, d), lambda i, dst: (i, 0)),
            pl.BlockSpec((rows, ROUTER_LANES), lambda i, dst: (i, 0)),
            pl.BlockSpec((1, N_MOD, d), lambda i, dst: (jnp.minimum(i * rows // seq, n_batch), 0, 0)),
            pl.BlockSpec((1, d), lambda i, dst: (0, 0)),
        ],
        out_specs=pl.BlockSpec((rows, d), lambda i, dst: (i, 0)),
        scratch_shapes=[pltpu.VMEM((2, TOP_K * rows, d), F32), pltpu.SemaphoreType.DMA((2,))],
    )
    return pl.pallas_call(
        functools.partial(_combine_kernel, final=final),
        grid_spec=grid_spec,
        out_shape=jax.ShapeDtypeStruct((n_rows, d), F32),
        compiler_params=_cparams("arbitrary"),
        name="moe_combine",
    )(dest, y_sorted, x_mid, wts, mod, g_final)


def _rope_tables(seq, pad_rows):
    rows = seq // GRID_W
    row = jnp.broadcast_to(jnp.arange(rows, dtype=I32)[:, None], (rows, GRID_W)).reshape(-1)
    col = jnp.broadcast_to(jnp.arange(GRID_W, dtype=I32)[None, :], (rows, GRID_W)).reshape(-1)
    axis_dim = HEAD_DIM // 2
    inv_freq = ROPE_THETA ** (-jnp.arange(0, axis_dim, 2, dtype=F32) / axis_dim)
    ang_r = row.astype(F32)[:, None] * inv_freq
    ang_c = col.astype(F32)[:, None] * inv_freq
    ang = jnp.concatenate([ang_r, ang_r, ang_c, ang_c], axis=-1)
    cos, sin = jnp.cos(ang), jnp.sin(ang)
    lo = (jnp.arange(HEAD_DIM) % (HEAD_DIM // 2)) < (HEAD_DIM // 4)
    sin_lo = jnp.where(lo[None, :], -sin, 0.0)
    sin_hi = jnp.where(lo[None, :], 0.0, sin)
    cos = jnp.concatenate([cos, jnp.ones((pad_rows, HEAD_DIM), F32)], axis=0)
    zeros = jnp.zeros((pad_rows, HEAD_DIM), F32)
    return cos, jnp.concatenate([sin_lo, zeros], axis=0), jnp.concatenate([sin_hi, zeros], axis=0)


def kernel(x, c, ctx, c_ctx, w_ada, b_ada, norm1_g, w_in, q_norm_g, k_norm_g, sconv_w, conf_dw_w,
           conf_dw_b, conf_ln_g, conf_ln_b, grp_norm_g, w_out, norm2_g, router_g_w, router_g_b,
           router_e_w, router_e_b, exp_w_gate, exp_w_up, exp_w_down, final_g):
    n_batch, seq, d = x.shape
    ctx_len = ctx.shape[1]
    depth = w_ada.shape[0]
    n_lat = n_batch * seq
    n_ctx = n_batch * ctx_len
    n_all = n_lat + n_ctx
    d_sconv = sconv_w.shape[2]
    assert n_batch + 1 <= MOD_ROWS and seq % TM_PROJ == 0 and n_ctx % TM_PROJ == 0
    assert seq % CONV_ROWS == 0 and ctx_len % CONV_ROWS == 0 and seq % GRID_W == 0
    assert n_lat % DISPATCH_ROWS == 0 and n_ctx % DISPATCH_ROWS == 0 and n_lat % ctx_len == 0
    assert N_GROUPS + N_EXPERTS <= ROUTER_LANES and seq % TQ_ATTN == 0 and ctx_len % TQ_ATTN == 0

    c_all = jnp.concatenate([c, c_ctx[None, :], jnp.zeros((MOD_ROWS - n_batch - 1, d), F32)], axis=0)
    mod_all = _ada_call(c_all, w_ada, b_ada).reshape(depth, MOD_ROWS, N_MOD, d)
    cos, sin_lo, sin_hi = _rope_tables(seq, TM_PROJ)
    row2 = lambda a: a.reshape(1, -1)
    lat_tiles = n_lat // TM_PROJ
    xa, xb, b_off = x.reshape(n_lat, d), ctx.reshape(n_ctx, d), 0

    for i in range(depth):
        last = i == depth - 1
        mod = mod_all[i]
        n_rows = n_lat if last else n_all
        q, k, v, gb, cu, glu = _inproj_call(
            xa, xb, b_off, mod, row2(norm1_g[i]), w_in[i].astype(BF16), row2(q_norm_g[i]),
            row2(k_norm_g[i]), cos, sin_lo, sin_hi, n_all=n_all, n_lat=n_lat, seq=seq)
        g_attn = row2(grp_norm_g[i][:D_ATTN])
        ya = _attn_call(q, k, v, g_attn, n_lat=n_lat, seq=seq, ctx_len=ctx_len, ctx_queries=not last)
        ys, yc = _conv_call(
            gb, cu, glu, sconv_w[i], conf_dw_w[i], row2(conf_dw_b[i]), row2(conf_ln_g[i]),
            row2(conf_ln_b[i]), row2(grp_norm_g[i][D_ATTN:D_ATTN + d_sconv]),
            row2(grp_norm_g[i][D_ATTN + d_sconv:]), n_rows=n_rows, n_lat=n_lat, seq=seq, ctx_len=ctx_len)
        n_pad = ROUTER_LANES - N_GROUPS - N_EXPERTS
        w_router = jnp.concatenate([router_g_w[i], router_e_w[i], jnp.zeros((d, n_pad), F32)],
                                   axis=1).astype(BF16)
        b_router = jnp.concatenate([router_g_b[i], router_e_b[i], jnp.zeros((n_pad,), F32)])[None, :]
        x_mid, h2, logits = _outproj_call(
            ya, ys, yc, xa, xb, b_off, mod, w_out[i].astype(BF16), row2(norm2_g[i]), w_router, b_router,
            n_rows=n_rows, n_lat=n_lat, seq=seq)
        idx, wts, cnt = _route_call(logits)
        dest, block_expert, n_used, pad_end, counts = _layout(idx, cnt)
        x_sorted = _dispatch_call(dest, pad_end, counts, h2, block_expert.shape[0] * MOE_ROWS)
        y_sorted = _moe_call(x_sorted, block_expert, n_used, exp_w_gate, exp_w_up, exp_w_down, i)
        x_all = _combine_call(dest, y_sorted, x_mid, wts, mod, row2(final_g),
                              n_rows=n_rows, n_lat=n_lat, seq=seq, final=last)
        xa, xb, b_off = x_all, x_all, lat_tiles
    return x_all.reshape(n_batch, seq, d)
```

```python
import functools

import jax
import jax.numpy as jnp
from jax import lax
from jax.experimental import pallas as pl
from jax.experimental.pallas import tpu as pltpu

F32 = jnp.float32
BF16 = jnp.bfloat16
I32 = jnp.int32

HEAD_DIM = 128
N_HEADS = 8
N_KV_HEADS = 2
GQA_GROUP = N_HEADS // N_KV_HEADS
D_ATTN = N_HEADS * HEAD_DIM
D_KV = N_KV_HEADS * HEAD_DIM
GRID_W = 64
ROPE_THETA = 10000.0
SCONV_WIDTH = 3
CONF_WIDTH = 31
N_GROUPS = 4
EXPERTS_PER_GROUP = 8
N_EXPERTS = N_GROUPS * EXPERTS_PER_GROUP
TOP_K = 2
N_MOD = 6
EPS = 1e-6
LOG2_E = 1.4426950408889634
SUBLANES = 8
DMA_UNROLL = 8

MOD_ROWS = 32
ROUTER_LANES = 128
TM_PROJ = 512
TQ_ATTN = 256
CONV_ROWS = 256
CONV_HALO = 16
ROUTE_ROWS = 512
DISPATCH_ROWS = 512
DISPATCH_BUFS = 3
MOE_ROWS = 512
COMBINE_ROWS = 512
ADA_TN = 1024
VMEM_LIMIT = 56 * 1024 * 1024


def _cparams(*sem):
    return pltpu.CompilerParams(dimension_semantics=sem, vmem_limit_bytes=VMEM_LIMIT)


def _dot(a, b):
    return jnp.dot(a, b, preferred_element_type=F32)


def _rms(x, g):
    return x * lax.rsqrt(jnp.mean(x * x, axis=-1, keepdims=True) + EPS) * g


def _ada_kernel(c_ref, w_ref, b_ref, o_ref):
    a = c_ref[...]
    a = a * jax.nn.sigmoid(a)
    o_ref[0] = _dot(a.astype(BF16), w_ref[0].astype(BF16)) + b_ref[0]


def _ada_call(c_all, w_ada, b_ada):
    depth, d, n = w_ada.shape
    return pl.pallas_call(
        _ada_kernel,
        grid=(depth, n // ADA_TN),
        in_specs=[
            pl.BlockSpec((MOD_ROWS, d), lambda l, j: (0, 0)),
            pl.BlockSpec((1, d, ADA_TN), lambda l, j: (l, 0, j)),
            pl.BlockSpec((1, 1, ADA_TN), lambda l, j: (l, 0, j)),
        ],
        out_specs=pl.BlockSpec((1, MOD_ROWS, ADA_TN), lambda l, j: (l, 0, j)),
        out_shape=jax.ShapeDtypeStruct((depth, MOD_ROWS, n), F32),
        compiler_params=_cparams("arbitrary", "arbitrary"),
        name="ada_mod",
    )(c_all, w_ada, b_ada.reshape(depth, 1, n))


def _rope(t, cos, sin_lo, sin_hi):
    return t * cos + pltpu.roll(t, 96, 1) * sin_lo + pltpu.roll(t, 32, 1) * sin_hi


def _inproj_kernel(xa_ref, xb_ref, mod_ref, g1_ref, w_ref, qg_ref, kg_ref, cos_ref, slo_ref, shi_ref,
                   q_ref, k_ref, v_ref, gb_ref, cu_ref, glu_ref, *, a_tiles, b_keys_only):
    x = jnp.where(pl.program_id(0) < a_tiles, xa_ref[...], xb_ref[...])
    shift = mod_ref[0, 0:1, :]
    scale = mod_ref[0, 1:2, :]
    h = _rms(x, g1_ref[...]) * (1.0 + scale) + shift
    hb = h.astype(BF16)
    cos, slo, shi = cos_ref[...], slo_ref[...], shi_ref[...]

    def queries_and_conv_inputs():
        qg = qg_ref[...]
        qscale = HEAD_DIM ** -0.5 * LOG2_E
        half = D_ATTN // 2
        for j in range(2):
            qc = _dot(hb, w_ref[:, j * half:(j + 1) * half])
            for hh in range(half // HEAD_DIM):
                t = _rms(qc[:, hh * HEAD_DIM:(hh + 1) * HEAD_DIM], qg)
                t = _rope(t, cos, slo, shi) * qscale
                q_ref[:, j * half + hh * HEAD_DIM: j * half + (hh + 1) * HEAD_DIM] = t.astype(BF16)
        c0 = D_ATTN + 2 * D_KV
        dg = gb_ref.shape[1]
        gb_ref[...] = _dot(hb, w_ref[:, c0:c0 + dg])
        cu_ref[...] = _dot(hb, w_ref[:, c0 + dg:c0 + 2 * dg]) * _dot(hb, w_ref[:, c0 + 2 * dg:c0 + 3 * dg])
        ca = _dot(hb, w_ref[:, c0 + 3 * dg:c0 + 4 * dg])
        cg = _dot(hb, w_ref[:, c0 + 4 * dg:c0 + 5 * dg])
        glu_ref[...] = ca * jax.nn.sigmoid(cg)

    if b_keys_only:
        pl.when(pl.program_id(0) < a_tiles)(queries_and_conv_inputs)
    else:
        queries_and_conv_inputs()

    kg = kg_ref[...]
    kv = _dot(hb, w_ref[:, D_ATTN:D_ATTN + 2 * D_KV])
    for hh in range(N_KV_HEADS):
        t = _rms(kv[:, hh * HEAD_DIM:(hh + 1) * HEAD_DIM], kg)
        k_ref[hh * HEAD_DIM:(hh + 1) * HEAD_DIM, :] = _rope(t, cos, slo, shi).T.astype(BF16)
    v_ref[...] = kv[:, D_KV:].astype(BF16)


def _two_source_specs(tm, d, a_tiles, b_off):
    return [pl.BlockSpec((tm, d), lambda i, *_: (jnp.minimum(i, a_tiles - 1), 0)),
            pl.BlockSpec((tm, d), lambda i, *_: (jnp.maximum(i - a_tiles, 0) + b_off, 0))]


def _inproj_call(xa, xb, b_off, mod, g1, w_in_bf, qg, kg, cos, slo, shi, *, n_all, n_lat, seq, ctx_keys_only):
    d = xa.shape[1]
    n_in = w_in_bf.shape[1]
    tm = TM_PROJ
    dg = (n_in - D_ATTN - 2 * D_KV) // 5
    lat_tiles = n_lat // tm
    seq_tiles = seq // tm
    n_batch = n_lat // seq

    def mod_map(i):
        return (jnp.minimum(i * tm // seq, n_batch), 0, 0)

    def rope_map(i):
        return (jnp.where(i < lat_tiles, i % seq_tiles, seq_tiles), 0)

    row = lambda i: (i, 0)
    fix = lambda i: (0, 0)
    lrow = (lambda i: (jnp.minimum(i, lat_tiles - 1), 0)) if ctx_keys_only else row
    n_q = n_lat if ctx_keys_only else n_all
    return pl.pallas_call(
        functools.partial(_inproj_kernel, a_tiles=lat_tiles, b_keys_only=ctx_keys_only),
        grid=(n_all // tm,),
        in_specs=_two_source_specs(tm, d, lat_tiles, b_off) + [
            pl.BlockSpec((1, N_MOD, d), mod_map),
            pl.BlockSpec((1, d), fix),
            pl.BlockSpec((d, n_in), fix, pipeline_mode=pl.Buffered(1)),
            pl.BlockSpec((1, HEAD_DIM), fix),
            pl.BlockSpec((1, HEAD_DIM), fix),
            pl.BlockSpec((tm, HEAD_DIM), rope_map),
            pl.BlockSpec((tm, HEAD_DIM), rope_map),
            pl.BlockSpec((tm, HEAD_DIM), rope_map),
        ],
        out_specs=[
            pl.BlockSpec((tm, D_ATTN), lrow),
            pl.BlockSpec((D_KV, tm), lambda i: (0, i)),
            pl.BlockSpec((tm, D_KV), row),
            pl.BlockSpec((tm, dg), lrow),
            pl.BlockSpec((tm, dg), lrow),
            pl.BlockSpec((tm, dg), lrow),
        ],
        out_shape=[
            jax.ShapeDtypeStruct((n_q, D_ATTN), BF16),
            jax.ShapeDtypeStruct((D_KV, n_all), BF16),
            jax.ShapeDtypeStruct((n_all, D_KV), BF16),
            jax.ShapeDtypeStruct((n_q, dg), F32),
            jax.ShapeDtypeStruct((n_q, dg), F32),
            jax.ShapeDtypeStruct((n_q, dg), F32),
        ],
        compiler_params=_cparams("arbitrary"),
        name="in_proj",
    )(xa, xb, mod, g1, w_in_bf, qg, kg, cos, slo, shi)


def _attend(q_ref, kv_refs, g_ref, o_ref):
    outs = []
    for hh in range(N_HEADS):
        kvh = hh // GQA_GROUP
        q = q_ref[:, hh * HEAD_DIM:(hh + 1) * HEAD_DIM]
        kts = [kt_ref[kvh * HEAD_DIM:(kvh + 1) * HEAD_DIM, :] for kt_ref, _ in kv_refs]
        vs = [v_ref[:, kvh * HEAD_DIM:(kvh + 1) * HEAD_DIM] for _, v_ref in kv_refs]
        ss = [_dot(q, kt) for kt in kts]
        m = ss[0].max(axis=-1, keepdims=True)
        for s in ss[1:]:
            m = jnp.maximum(m, s.max(axis=-1, keepdims=True))
        ps = [jnp.exp2(s - m) for s in ss]
        den = ps[0].sum(axis=-1, keepdims=True)
        for p in ps[1:]:
            den = den + p.sum(axis=-1, keepdims=True)
        o = _dot(ps[0].astype(BF16), vs[0])
        for p, v in zip(ps[1:], vs[1:]):
            o = o + _dot(p.astype(BF16), v)
        outs.append(o / den)
    y = jnp.concatenate(outs, axis=-1)
    o_ref[...] = _rms(y, g_ref[...]).astype(BF16)


def _attn_kernel(q_ref, kl_ref, vl_ref, kc_ref, vc_ref, g_ref, o_ref, *, lat_steps, ctx_steps):
    if ctx_steps == 0:
        _attend(q_ref, [(kl_ref, vl_ref), (kc_ref, vc_ref)], g_ref, o_ref)
        return
    i = pl.program_id(1)

    @pl.when(i < lat_steps)
    def _():
        _attend(q_ref, [(kl_ref, vl_ref), (kc_ref, vc_ref)], g_ref, o_ref)

    @pl.when(i >= lat_steps)
    def _():
        _attend(q_ref, [(kc_ref, vc_ref)], g_ref, o_ref)


def _attn_call(q, k, v, g_attn, *, n_lat, seq, ctx_len, ctx_queries):
    tq = TQ_ATTN
    n_batch = n_lat // seq
    qt = seq // tq
    ct = ctx_len // tq if ctx_queries else 0
    n_out = n_lat + (n_batch * ctx_len if ctx_queries else 0)

    def q_map(b, i):
        return (jnp.where(i < qt, b * qt + i, n_lat // tq + b * ct + (i - qt)), 0)

    lat_v = pl.BlockSpec((seq, D_KV), lambda b, i: (b, 0))
    ctx_v = pl.BlockSpec((ctx_len, D_KV), lambda b, i: (n_lat // ctx_len + b, 0))
    lat_kt = pl.BlockSpec((D_KV, seq), lambda b, i: (0, b))
    ctx_kt = pl.BlockSpec((D_KV, ctx_len), lambda b, i: (0, n_lat // ctx_len + b))
    return pl.pallas_call(
        functools.partial(_attn_kernel, lat_steps=qt, ctx_steps=ct),
        grid=(n_batch, qt + ct),
        in_specs=[pl.BlockSpec((tq, D_ATTN), q_map), lat_kt, lat_v, ctx_kt, ctx_v,
                  pl.BlockSpec((1, D_ATTN), lambda b, i: (0, 0))],
        out_specs=pl.BlockSpec((tq, D_ATTN), q_map),
        out_shape=jax.ShapeDtypeStruct((n_out, D_ATTN), BF16),
        compiler_params=_cparams("arbitrary", "arbitrary"),
        name="attention",
    )(q, k, v, k, v, g_attn)


def _conv_taps(win, shifted, w_ref, first, width, rows):
    span = win.shape[0] - SUBLANES
    acc = None
    for j in range(SUBLANES):
        taps = [t for t in range(width) if (first + t) % SUBLANES == j]
        if not taps:
            continue
        if j == 0:
            src = win
        else:
            shifted[j - 1, :, :] = win[j:j + span, :]
            src = shifted.at[j - 1]
        for t in taps:
            a = first + t - j
            term = w_ref[t:t + 1, :] * src[a:a + rows, :]
            acc = term if acc is None else acc + term
    return acc


def _conv_kernel(gb_ref, cu_ref, cu_p_ref, cu_n_ref, gl_ref, gl_p_ref, gl_n_ref,
                 sw_ref, dw_ref, db_ref, lg_ref, lb_ref, gs_ref, gc_ref,
                 ys_ref, yc_ref, win_s, win_c, shifted, *, lat_chunks, seq_chunks, ctx_chunks):
    i = pl.program_id(0)
    r = CONV_ROWS
    hl = CONV_HALO
    pos = jnp.where(i < lat_chunks, i % seq_chunks, (i - lat_chunks) % ctx_chunks)
    last = jnp.where(i < lat_chunks, seq_chunks - 1, ctx_chunks - 1)
    keep_p = (pos > 0).astype(F32)
    keep_n = (pos < last).astype(F32)

    win_s[0:hl, :] = cu_p_ref[...] * keep_p
    win_s[hl:hl + r, :] = cu_ref[...]
    win_s[hl + r:, :] = cu_n_ref[...] * keep_n
    win_c[0:hl, :] = gl_p_ref[...] * keep_p
    win_c[hl:hl + r, :] = gl_ref[...]
    win_c[hl + r:, :] = gl_n_ref[...] * keep_n

    acc = _conv_taps(win_s, shifted, sw_ref, hl - (SCONV_WIDTH - 1) // 2, SCONV_WIDTH, r)
    ys_ref[...] = _rms(gb_ref[...] * acc, gs_ref[...]).astype(BF16)

    acc = _conv_taps(win_c, shifted, dw_ref, hl - (CONF_WIDTH - 1) // 2, CONF_WIDTH, r)
    u = acc + db_ref[...]
    mu = jnp.mean(u, axis=-1, keepdims=True)
    uc = u - mu
    var = jnp.mean(uc * uc, axis=-1, keepdims=True)
    z = uc * lax.rsqrt(var + EPS) * lg_ref[...] + lb_ref[...]
    z = z * jax.nn.sigmoid(z)
    yc_ref[...] = _rms(z, gc_ref[...]).astype(BF16)


def _conv_call(gb, cu, glu, sconv_w, dw_w, dw_b, ln_g, ln_b, g_s, g_c, *, n_rows, n_lat, seq, ctx_len):
    dg = gb.shape[1]
    r, hl = CONV_ROWS, CONV_HALO
    n_chunks = n_rows // r
    per = r // hl
    n_halo = n_rows // hl
    cur = lambda i: (i, 0)
    prev = lambda i: (jnp.maximum(i * per - 1, 0), 0)
    nxt = lambda i: (jnp.minimum((i + 1) * per, n_halo - 1), 0)
    fix = lambda i: (0, 0)
    kern = functools.partial(_conv_kernel, lat_chunks=n_lat // r, seq_chunks=seq // r,
                             ctx_chunks=max(ctx_len // r, 1))
    return pl.pallas_call(
        kern,
        grid=(n_chunks,),
        in_specs=[
            pl.BlockSpec((r, dg), cur),
            pl.BlockSpec((r, dg), cur), pl.BlockSpec((hl, dg), prev), pl.BlockSpec((hl, dg), nxt),
            pl.BlockSpec((r, dg), cur), pl.BlockSpec((hl, dg), prev), pl.BlockSpec((hl, dg), nxt),
            pl.BlockSpec((SCONV_WIDTH, dg), fix),
            pl.BlockSpec((CONF_WIDTH, dg), fix),
            pl.BlockSpec((1, dg), fix), pl.BlockSpec((1, dg), fix), pl.BlockSpec((1, dg), fix),
            pl.BlockSpec((1, dg), fix), pl.BlockSpec((1, dg), fix),
        ],
        out_specs=[pl.BlockSpec((r, dg), cur), pl.BlockSpec((r, dg), cur)],
        out_shape=[jax.ShapeDtypeStruct((n_rows, dg), BF16), jax.ShapeDtypeStruct((n_rows, dg), BF16)],
        scratch_shapes=[pltpu.VMEM((r + 2 * hl, dg), F32), pltpu.VMEM((r + 2 * hl, dg), F32),
                        pltpu.VMEM((SUBLANES - 1, r + 2 * hl - SUBLANES, dg), F32)],
        compiler_params=_cparams("arbitrary"),
        name="group_convs",
    )(gb, cu, cu, cu, glu, glu, glu, sconv_w, dw_w, dw_b, ln_g, ln_b, g_s, g_c)


def _outproj_kernel(ya_ref, ys_ref, yc_ref, xa_ref, xb_ref, mod_ref, wo_ref, g2_ref, wr_ref, br_ref,
                    xo_ref, h2_ref, lg_ref, *, a_tiles):
    da = ya_ref.shape[1]
    ds_ = ys_ref.shape[1]
    mix = _dot(ya_ref[...], wo_ref[0:da, :])
    mix = mix + _dot(ys_ref[...], wo_ref[da:da + ds_, :])
    mix = mix + _dot(yc_ref[...], wo_ref[da + ds_:, :])
    x = jnp.where(pl.program_id(0) < a_tiles, xa_ref[...], xb_ref[...])
    x = x + mod_ref[0, 2:3, :] * mix
    xo_ref[...] = x
    h2 = _rms(x, g2_ref[...]) * (1.0 + mod_ref[0, 4:5, :]) + mod_ref[0, 3:4, :]
    h2_ref[...] = h2
    lg_ref[...] = _dot(h2.astype(BF16), wr_ref[...]) + br_ref[...]


def _outproj_call(ya, ys, yc, xa, xb, b_off, mod, w_out_bf, g2, w_router, b_router, *, n_rows, n_lat, seq):
    d = xa.shape[1]
    tm = TM_PROJ
    n_batch = n_lat // seq
    row = lambda i: (i, 0)
    fix = lambda i: (0, 0)
    mod_map = lambda i: (jnp.minimum(i * tm // seq, n_batch), 0, 0)
    return pl.pallas_call(
        functools.partial(_outproj_kernel, a_tiles=n_lat // tm),
        grid=(n_rows // tm,),
        in_specs=[
            pl.BlockSpec((tm, ya.shape[1]), row),
            pl.BlockSpec((tm, ys.shape[1]), row),
            pl.BlockSpec((tm, yc.shape[1]), row),
        ] + _two_source_specs(tm, d, n_lat // tm, b_off) + [
            pl.BlockSpec((1, N_MOD, d), mod_map),
            pl.BlockSpec(w_out_bf.shape, fix, pipeline_mode=pl.Buffered(1)),
            pl.BlockSpec((1, d), fix),
            pl.BlockSpec((d, ROUTER_LANES), fix),
            pl.BlockSpec((1, ROUTER_LANES), fix),
        ],
        out_specs=[pl.BlockSpec((tm, d), row), pl.BlockSpec((tm, d), row),
                   pl.BlockSpec((tm, ROUTER_LANES), row)],
        out_shape=[jax.ShapeDtypeStruct((n_rows, d), F32), jax.ShapeDtypeStruct((n_rows, d), F32),
                   jax.ShapeDtypeStruct((n_rows, ROUTER_LANES), F32)],
        compiler_params=_cparams("arbitrary"),
        name="out_proj",
    )(ya, ys, yc, xa, xb, mod, w_out_bf, g2, w_router, b_router)


def _first_lane_of_max(vals, lane_f):
    m = vals.max(axis=-1, keepdims=True)
    return m, jnp.where(vals == m, lane_f, float(ROUTER_LANES)).min(axis=-1, keepdims=True)


def _route_kernel(lg_ref, idx_ref, wt_ref, cnt_ref, carry):
    i = pl.program_id(0)
    tr = lg_ref.shape[0]

    @pl.when(i == 0)
    def _():
        carry[...] = jnp.zeros_like(carry)

    lg = lg_ref[...]
    lane = lax.broadcasted_iota(I32, lg.shape, 1)
    lane_f = lane.astype(F32)
    neg = -jnp.inf
    gmask = lane < N_GROUPS
    gl = jnp.where(gmask, lg, neg)
    gmax, g_idx = _first_lane_of_max(gl, lane_f)
    g_top = 1.0 / jnp.where(gmask, jnp.exp(gl - gmax), 0.0).sum(axis=-1, keepdims=True)
    lo = float(N_GROUPS) + g_idx * float(EXPERTS_PER_GROUP)
    emask = jnp.logical_and(lane_f >= lo, lane_f < lo + float(EXPERTS_PER_GROUP))
    el = jnp.where(emask, lg, neg)
    m1, l1 = _first_lane_of_max(el, lane_f)
    o1 = lane_f == l1
    el2 = jnp.where(o1, neg, el)
    m2, l2 = _first_lane_of_max(el2, lane_f)
    o2 = lane_f == l2
    ratio = jnp.exp(m2 - m1)
    w1 = g_top / (1.0 + ratio)
    w2 = w1 * ratio

    onehot = jnp.where(jnp.logical_or(o1, o2), 1.0, 0.0)
    rows_i = lax.broadcasted_iota(I32, (tr, tr), 0)
    cols_i = lax.broadcasted_iota(I32, (tr, tr), 1)
    below = jnp.where(cols_i < rows_i, 1.0, 0.0).astype(BF16)
    base = carry[...] + _dot(below, onehot.astype(BF16))
    r1 = jnp.where(o1, base, 0.0).sum(axis=-1, keepdims=True)
    r2 = jnp.where(o2, base, 0.0).sum(axis=-1, keepdims=True)
    carry[...] = carry[...] + onehot.sum(axis=0, keepdims=True)
    cnt_ref[...] = jnp.broadcast_to(carry[...], cnt_ref.shape)

    e1 = l1 - float(N_GROUPS)
    e2 = l2 - float(N_GROUPS)
    idx = jnp.where(lane == 0, e1, jnp.where(lane == 1, e2, jnp.where(lane == 2, r1, r2)))
    idx_ref[...] = idx.astype(I32)
    wt_ref[...] = jnp.where(lane == 0, w1, w2)


def _route_call(logits):
    n_tok = logits.shape[0]
    tr = ROUTE_ROWS
    row = lambda i: (i, 0)
    return pl.pallas_call(
        _route_kernel,
        grid=(n_tok // tr,),
        in_specs=[pl.BlockSpec((tr, ROUTER_LANES), row)],
        out_specs=[pl.BlockSpec((tr, ROUTER_LANES), row), pl.BlockSpec((tr, ROUTER_LANES), row),
                   pl.BlockSpec((8, ROUTER_LANES), lambda i: (0, 0))],
        out_shape=[jax.ShapeDtypeStruct((n_tok, ROUTER_LANES), I32),
                   jax.ShapeDtypeStruct((n_tok, ROUTER_LANES), F32),
                   jax.ShapeDtypeStruct((8, ROUTER_LANES), F32)],
        scratch_shapes=[pltpu.VMEM((1, ROUTER_LANES), F32)],
        compiler_params=_cparams("arbitrary"),
        name="route",
    )(logits)


def _layout(idx, cnt):
    n_tok = idx.shape[0]
    rows = MOE_ROWS
    n_blocks = n_tok * TOP_K // rows + N_EXPERTS
    counts = cnt[0, N_GROUPS:N_GROUPS + N_EXPERTS].astype(I32)
    padded = (counts + rows - 1) // rows * rows
    pad_end = jnp.cumsum(padded)
    pad_start = pad_end - padded
    expert = idx[:, 0:TOP_K]
    rank = idx[:, TOP_K:2 * TOP_K]
    start = jnp.sum(jnp.where(expert[:, :, None] == jnp.arange(N_EXPERTS, dtype=I32), pad_start, 0), axis=-1)
    dest = (start + rank).reshape(-1).astype(I32)
    first_row = jnp.arange(n_blocks, dtype=I32) * rows
    block_expert = jnp.minimum(jnp.sum(pad_end[None, :] <= first_row[:, None], axis=1), N_EXPERTS - 1)
    n_used = (pad_end[-1] // rows).reshape(1)
    return dest, block_expert.astype(I32), n_used.astype(I32), pad_end.astype(I32), counts


def _dispatch_kernel(dst_ref, pend_ref, cnt_ref, h_hbm, xs_hbm, zbuf, hbuf, sem_in, sem_out, zsem):
    i = pl.program_id(0)
    n = pl.num_programs(0)
    td = DISPATCH_ROWS

    def load(step, slot):
        rows = pl.ds(pl.multiple_of(step * td, td), td)
        return pltpu.make_async_copy(h_hbm.at[rows, :], hbuf.at[slot], sem_in.at[slot])

    def drain(slot):
        for kk in range(TOP_K):
            pltpu.make_async_copy(hbuf.at[slot], xs_hbm.at[pl.ds(0, td), :], sem_out.at[slot]).wait()

    def zero_copy(e):
        start = pl.multiple_of(pend_ref[e] - MOE_ROWS, MOE_ROWS)
        return pltpu.make_async_copy(zbuf, xs_hbm.at[pl.ds(start, MOE_ROWS), :], zsem)

    @pl.when(i == 0)
    def _():
        zbuf[...] = jnp.zeros_like(zbuf)

        def start(e, c):
            @pl.when(cnt_ref[e] > 0)
            def _():
                zero_copy(e).start()
            return c

        def wait(e, c):
            @pl.when(cnt_ref[e] > 0)
            def _():
                zero_copy(e).wait()
            return c

        lax.fori_loop(0, N_EXPERTS, start, 0)
        lax.fori_loop(0, N_EXPERTS, wait, 0)

        def tail_copy(b):
            return pltpu.make_async_copy(zbuf, xs_hbm.at[pl.ds(pl.multiple_of(b * MOE_ROWS, MOE_ROWS), MOE_ROWS), :],
                                         zsem)

        n_blocks = xs_hbm.shape[0] // MOE_ROWS
        first_free = pend_ref[N_EXPERTS - 1] // MOE_ROWS
        lax.fori_loop(first_free, n_blocks, lambda b, c: (tail_copy(b).start(), c)[1], 0)
        lax.fori_loop(first_free, n_blocks, lambda b, c: (tail_copy(b).wait(), c)[1], 0)
        load(0, 0).start()

    slot = i % DISPATCH_BUFS
    nxt = (i + 1) % DISPATCH_BUFS

    @pl.when(i + 1 < n)
    def _():
        @pl.when(i + 1 >= DISPATCH_BUFS)
        def _():
            drain(nxt)
        load(i + 1, nxt).start()

    load(i, slot).wait()
    src_tile = hbuf.at[slot]
    out_sem = sem_out.at[slot]

    def group(g, c):
        base = pl.multiple_of(g * DMA_UNROLL, DMA_UNROLL)
        for u in range(DMA_UNROLL):
            src = src_tile.at[pl.ds(base + u, 1), :]
            for kk in range(TOP_K):
                row = dst_ref[TOP_K * (i * td + base + u) + kk]
                pltpu.make_async_copy(src, xs_hbm.at[pl.ds(row, 1), :], out_sem).start()
        return c

    lax.fori_loop(0, td // DMA_UNROLL, group, 0)

    @pl.when(i == n - 1)
    def _():
        for back in range(DISPATCH_BUFS):
            @pl.when(i >= back)
            def _():
                drain((i - back) % DISPATCH_BUFS)


def _dispatch_call(dest, pad_end, counts, h2, n_sorted_rows):
    n_tok, d = h2.shape
    td = DISPATCH_ROWS
    grid_spec = pltpu.PrefetchScalarGridSpec(
        num_scalar_prefetch=3,
        grid=(n_tok // td,),
        in_specs=[pl.BlockSpec(memory_space=pl.ANY)],
        out_specs=pl.BlockSpec(memory_space=pl.ANY),
        scratch_shapes=[pltpu.VMEM((MOE_ROWS, d), h2.dtype), pltpu.VMEM((DISPATCH_BUFS, td, d), h2.dtype),
                        pltpu.SemaphoreType.DMA((DISPATCH_BUFS,)), pltpu.SemaphoreType.DMA((DISPATCH_BUFS,)),
                        pltpu.SemaphoreType.DMA(())],
    )
    return pl.pallas_call(
        _dispatch_kernel,
        grid_spec=grid_spec,
        out_shape=jax.ShapeDtypeStruct((n_sorted_rows, d), h2.dtype),
        compiler_params=_cparams("arbitrary"),
        name="moe_dispatch",
    )(dest, pad_end, counts, h2)


def _moe_kernel(be_ref, used_ref, x_ref, wg_ref, wu_ref, wd_ref, y_ref, wgb, wub, wdb):
    b = pl.program_id(0)

    @pl.when(b < used_ref[0])
    def _():
        prev = be_ref[jnp.maximum(b - 1, 0)]

        @pl.when(jnp.logical_or(b == 0, be_ref[b] != prev))
        def _():
            wgb[...] = wg_ref[0, 0].astype(BF16)
            wub[...] = wu_ref[0, 0].astype(BF16)
            wdb[...] = wd_ref[0, 0].astype(BF16)

        xb = x_ref[...].astype(BF16)
        gate = _dot(xb, wgb[...])
        up = _dot(xb, wub[...])
        act = (gate * jax.nn.sigmoid(gate) * up).astype(BF16)
        y_ref[...] = _dot(act, wdb[...])

    @pl.when(b >= used_ref[0])
    def _():
        y_ref[...] = jnp.zeros_like(y_ref)


def _moe_call(x_sorted, block_expert, n_used, w_gate, w_up, w_down, layer):
    n_blocks = block_expert.shape[0]
    rows = MOE_ROWS
    _, n_exp, d, de = w_gate.shape

    def w_map(b, be, used):
        return (layer, be[jnp.minimum(b, used[0] - 1)], 0, 0)

    grid_spec = pltpu.PrefetchScalarGridSpec(
        num_scalar_prefetch=2,
        grid=(n_blocks,),
        in_specs=[
            pl.BlockSpec((rows, d), lambda b, be, used: (jnp.minimum(b, used[0] - 1), 0)),
            pl.BlockSpec((1, 1, d, de), w_map),
            pl.BlockSpec((1, 1, d, de), w_map),
            pl.BlockSpec((1, 1, de, d), w_map),
        ],
        out_specs=pl.BlockSpec((rows, d), lambda b, be, used: (b, 0)),
        scratch_shapes=[
            pltpu.VMEM((d, de), BF16),
            pltpu.VMEM((d, de), BF16),
            pltpu.VMEM((de, d), BF16),
        ],
    )
    return pl.pallas_call(
        _moe_kernel,
        grid_spec=grid_spec,
        out_shape=jax.ShapeDtypeStruct((n_blocks * rows, d), F32),
        compiler_params=_cparams("arbitrary"),
        name="moe_experts",
    )(block_expert, n_used, x_sorted, w_gate, w_up, w_down)


def _combine_kernel(dst_ref, y_hbm, x_ref, wt_ref, mod_ref, g_ref, o_ref, ybuf, sem, *, final):
    i = pl.program_id(0)
    n = pl.num_programs(0)
    rows = COMBINE_ROWS

    def gather(step, slot):
        def group(g, c):
            base = pl.multiple_of(g * DMA_UNROLL, DMA_UNROLL)
            for u in range(DMA_UNROLL):
                for kk in range(TOP_K):
                    row = dst_ref[TOP_K * (step * rows + base + u) + kk]
                    pltpu.make_async_copy(y_hbm.at[pl.ds(row, 1), :],
                                          ybuf.at[slot, pl.ds(kk * rows + base + u, 1), :], sem.at[slot]).start()
            return c
        lax.fori_loop(0, rows // DMA_UNROLL, group, 0)

    slot = i % 2

    @pl.when(i == 0)
    def _():
        gather(0, 0)

    @pl.when(i + 1 < n)
    def _():
        gather(i + 1, 1 - slot)

    pltpu.make_async_copy(y_hbm.at[pl.ds(0, TOP_K * rows), :], ybuf.at[slot], sem.at[slot]).wait()
    f = wt_ref[:, 0:1] * ybuf[slot, 0:rows, :]
    for kk in range(1, TOP_K):
        f = f + wt_ref[:, kk:kk + 1] * ybuf[slot, kk * rows:(kk + 1) * rows, :]
    x = x_ref[...] + mod_ref[0, 5:6, :] * f
    if final:
        x = _rms(x, g_ref[...])
    o_ref[...] = x


def _combine_call(dest, y_sorted, x_mid, wts, mod, g_final, *, n_rows, n_lat, seq, final):
    d = x_mid.shape[1]
    rows = COMBINE_ROWS
    n_batch = n_lat // seq
    grid_spec = pltpu.PrefetchScalarGridSpec(
        num_scalar_prefetch=1,
        grid=(n_rows // rows,),
        in_specs=[
            pl.BlockSpec(memory_space=pl.ANY),
            pl.BlockSpec((rows, d), lambda i, dst: (i, 0)),
            pl.BlockSpec((rows, ROUTER_LANES), lambda i, dst: (i, 0)),
            pl.BlockSpec((1, N_MOD, d), lambda i, dst: (jnp.minimum(i * rows // seq, n_batch), 0, 0)),
            pl.BlockSpec((1, d), lambda i, dst: (0, 0)),
        ],
        out_specs=pl.BlockSpec((rows, d), lambda i, dst: (i, 0)),
        scratch_shapes=[pltpu.VMEM((2, TOP_K * rows, d), F32), pltpu.SemaphoreType.DMA((2,))],
    )
    return pl.pallas_call(
        functools.partial(_combine_kernel, final=final),
        grid_spec=grid_spec,
        out_shape=jax.ShapeDtypeStruct((n_rows, d), F32),
        compiler_params=_cparams("arbitrary"),
        name="moe_combine",
    )(dest, y_sorted, x_mid, wts, mod, g_final)


def _rope_tables(seq, pad_rows):
    rows = seq // GRID_W
    row = jnp.broadcast_to(jnp.arange(rows, dtype=I32)[:, None], (rows, GRID_W)).reshape(-1)
    col = jnp.broadcast_to(jnp.arange(GRID_W, dtype=I32)[None, :], (rows, GRID_W)).reshape(-1)
    axis_dim = HEAD_DIM // 2
    inv_freq = ROPE_THETA ** (-jnp.arange(0, axis_dim, 2, dtype=F32) / axis_dim)
    ang_r = row.astype(F32)[:, None] * inv_freq
    ang_c = col.astype(F32)[:, None] * inv_freq
    ang = jnp.concatenate([ang_r, ang_r, ang_c, ang_c], axis=-1)
    cos, sin = jnp.cos(ang), jnp.sin(ang)
    lo = (jnp.arange(HEAD_DIM) % (HEAD_DIM // 2)) < (HEAD_DIM // 4)
    sin_lo = jnp.where(lo[None, :], -sin, 0.0)
    sin_hi = jnp.where(lo[None, :], 0.0, sin)
    cos = jnp.concatenate([cos, jnp.ones((pad_rows, HEAD_DIM), F32)], axis=0)
    zeros = jnp.zeros((pad_rows, HEAD_DIM), F32)
    return cos, jnp.concatenate([sin_lo, zeros], axis=0), jnp.concatenate([sin_hi, zeros], axis=0)


def kernel(x, c, ctx, c_ctx, w_ada, b_ada, norm1_g, w_in, q_norm_g, k_norm_g, sconv_w, conf_dw_w,
           conf_dw_b, conf_ln_g, conf_ln_b, grp_norm_g, w_out, norm2_g, router_g_w, router_g_b,
           router_e_w, router_e_b, exp_w_gate, exp_w_up, exp_w_down, final_g):
    n_batch, seq, d = x.shape
    ctx_len = ctx.shape[1]
    depth = w_ada.shape[0]
    n_lat = n_batch * seq
    n_ctx = n_batch * ctx_len
    n_all = n_lat + n_ctx
    d_sconv = sconv_w.shape[2]
    assert n_batch + 1 <= MOD_ROWS and seq % TM_PROJ == 0 and n_ctx % TM_PROJ == 0
    assert seq % CONV_ROWS == 0 and ctx_len % CONV_ROWS == 0 and seq % GRID_W == 0
    assert n_lat % DISPATCH_ROWS == 0 and n_ctx % DISPATCH_ROWS == 0 and n_lat % ctx_len == 0
    assert n_lat % COMBINE_ROWS == 0 and n_ctx % COMBINE_ROWS == 0
    assert N_GROUPS + N_EXPERTS <= ROUTER_LANES and seq % TQ_ATTN == 0 and ctx_len % TQ_ATTN == 0

    c_all = jnp.concatenate([c, c_ctx[None, :], jnp.zeros((MOD_ROWS - n_batch - 1, d), F32)], axis=0)
    mod_all = _ada_call(c_all, w_ada, b_ada).reshape(depth, MOD_ROWS, N_MOD, d)
    cos, sin_lo, sin_hi = _rope_tables(seq, TM_PROJ)
    row2 = lambda a: a.reshape(1, -1)
    lat_tiles = n_lat // TM_PROJ
    xa, xb, b_off = x.reshape(n_lat, d), ctx.reshape(n_ctx, d), 0

    for i in range(depth):
        last = i == depth - 1
        mod = mod_all[i]
        n_rows = n_lat if last else n_all
        q, k, v, gb, cu, glu = _inproj_call(
            xa, xb, b_off, mod, row2(norm1_g[i]), w_in[i].astype(BF16), row2(q_norm_g[i]),
            row2(k_norm_g[i]), cos, sin_lo, sin_hi, n_all=n_all, n_lat=n_lat, seq=seq, ctx_keys_only=last)
        g_attn = row2(grp_norm_g[i][:D_ATTN])
        ya = _attn_call(q, k, v, g_attn, n_lat=n_lat, seq=seq, ctx_len=ctx_len, ctx_queries=not last)
        ys, yc = _conv_call(
            gb, cu, glu, sconv_w[i], conf_dw_w[i], row2(conf_dw_b[i]), row2(conf_ln_g[i]),
            row2(conf_ln_b[i]), row2(grp_norm_g[i][D_ATTN:D_ATTN + d_sconv]),
            row2(grp_norm_g[i][D_ATTN + d_sconv:]), n_rows=n_rows, n_lat=n_lat, seq=seq, ctx_len=ctx_len)
        n_pad = ROUTER_LANES - N_GROUPS - N_EXPERTS
        w_router = jnp.concatenate([router_g_w[i], router_e_w[i], jnp.zeros((d, n_pad), F32)],
                                   axis=1).astype(BF16)
        b_router = jnp.concatenate([router_g_b[i], router_e_b[i], jnp.zeros((n_pad,), F32)])[None, :]
        x_mid, h2, logits = _outproj_call(
            ya, ys, yc, xa, xb, b_off, mod, w_out[i].astype(BF16), row2(norm2_g[i]), w_router, b_router,
            n_rows=n_rows, n_lat=n_lat, seq=seq)
        idx, wts, cnt = _route_call(logits)
        dest, block_expert, n_used, pad_end, counts = _layout(idx, cnt)
        x_sorted = _dispatch_call(dest, pad_end, counts, h2, block_expert.shape[0] * MOE_ROWS)
        y_sorted = _moe_call(x_sorted, block_expert, n_used, exp_w_gate, exp_w_up, exp_w_down, i)
        x_all = _combine_call(dest, y_sorted, x_mid, wts, mod, row2(final_g),
                              n_rows=n_rows, n_lat=n_lat, seq=seq, final=last)
        xa, xb, b_off = x_all, x_all, lat_tiles
    return x_all.reshape(n_batch, seq, d)
```

```python
import functools

import jax
import jax.numpy as jnp
from jax import lax
from jax.experimental import pallas as pl
from jax.experimental.pallas import tpu as pltpu

F32 = jnp.float32
BF16 = jnp.bfloat16
I32 = jnp.int32

HEAD_DIM = 128
N_HEADS = 8
N_KV_HEADS = 2
GQA_GROUP = N_HEADS // N_KV_HEADS
D_ATTN = N_HEADS * HEAD_DIM
D_KV = N_KV_HEADS * HEAD_DIM
GRID_W = 64
ROPE_THETA = 10000.0
SCONV_WIDTH = 3
CONF_WIDTH = 31
N_GROUPS = 4
EXPERTS_PER_GROUP = 8
N_EXPERTS = N_GROUPS * EXPERTS_PER_GROUP
TOP_K = 2
N_MOD = 6
EPS = 1e-6
LOG2_E = 1.4426950408889634
SUBLANES = 8
DMA_UNROLL = 8

MOD_ROWS = 32
ROUTER_LANES = 128
TM_PROJ = 512
TQ_ATTN = 256
CONV_ROWS = 256
CONV_HALO = 16
ROUTE_ROWS = 512
DISPATCH_ROWS = 512
DISPATCH_BUFS = 3
MOE_ROWS = 512
COMBINE_ROWS = 512
ADA_TN = 1024
VMEM_LIMIT = 56 * 1024 * 1024


def _cparams(*sem):
    return pltpu.CompilerParams(dimension_semantics=sem, vmem_limit_bytes=VMEM_LIMIT)


def _dot(a, b):
    return jnp.dot(a, b, preferred_element_type=F32)


def _rms(x, g):
    return x * lax.rsqrt(jnp.mean(x * x, axis=-1, keepdims=True) + EPS) * g


def _ada_kernel(c_ref, w_ref, b_ref, o_ref):
    a = c_ref[...]
    a = a * jax.nn.sigmoid(a)
    o_ref[0] = _dot(a.astype(BF16), w_ref[0].astype(BF16)) + b_ref[0]


def _ada_call(c_all, w_ada, b_ada):
    depth, d, n = w_ada.shape
    return pl.pallas_call(
        _ada_kernel,
        grid=(depth, n // ADA_TN),
        in_specs=[
            pl.BlockSpec((MOD_ROWS, d), lambda l, j: (0, 0)),
            pl.BlockSpec((1, d, ADA_TN), lambda l, j: (l, 0, j)),
            pl.BlockSpec((1, 1, ADA_TN), lambda l, j: (l, 0, j)),
        ],
        out_specs=pl.BlockSpec((1, MOD_ROWS, ADA_TN), lambda l, j: (l, 0, j)),
        out_shape=jax.ShapeDtypeStruct((depth, MOD_ROWS, n), F32),
        compiler_params=_cparams("arbitrary", "arbitrary"),
        name="ada_mod",
    )(c_all, w_ada, b_ada.reshape(depth, 1, n))


def _rope(t, cos, sin_lo, sin_hi):
    return t * cos + pltpu.roll(t, 96, 1) * sin_lo + pltpu.roll(t, 32, 1) * sin_hi


def _inproj_kernel(xa_ref, xb_ref, mod_ref, g1_ref, w_ref, qg_ref, kg_ref, cos_ref, slo_ref, shi_ref,
                   q_ref, k_ref, v_ref, gb_ref, cu_ref, glu_ref, *, a_tiles):
    x = jnp.where(pl.program_id(0) < a_tiles, xa_ref[...], xb_ref[...])
    shift = mod_ref[0, 0:1, :]
    scale = mod_ref[0, 1:2, :]
    h = _rms(x, g1_ref[...]) * (1.0 + scale) + shift
    hb = h.astype(BF16)
    cos, slo, shi = cos_ref[...], slo_ref[...], shi_ref[...]
    qg, kg = qg_ref[...], kg_ref[...]
    qscale = HEAD_DIM ** -0.5 * LOG2_E
    half = D_ATTN // 2
    for j in range(2):
        qc = _dot(hb, w_ref[:, j * half:(j + 1) * half])
        for hh in range(half // HEAD_DIM):
            t = _rms(qc[:, hh * HEAD_DIM:(hh + 1) * HEAD_DIM], qg)
            t = _rope(t, cos, slo, shi) * qscale
            q_ref[:, j * half + hh * HEAD_DIM: j * half + (hh + 1) * HEAD_DIM] = t.astype(BF16)
    kv = _dot(hb, w_ref[:, D_ATTN:D_ATTN + 2 * D_KV])
    for hh in range(N_KV_HEADS):
        t = _rms(kv[:, hh * HEAD_DIM:(hh + 1) * HEAD_DIM], kg)
        k_ref[hh * HEAD_DIM:(hh + 1) * HEAD_DIM, :] = _rope(t, cos, slo, shi).T.astype(BF16)
    v_ref[...] = kv[:, D_KV:].astype(BF16)
    c0 = D_ATTN + 2 * D_KV
    dg = gb_ref.shape[1]
    gb_ref[...] = _dot(hb, w_ref[:, c0:c0 + dg])
    cu_ref[...] = _dot(hb, w_ref[:, c0 + dg:c0 + 2 * dg]) * _dot(hb, w_ref[:, c0 + 2 * dg:c0 + 3 * dg])
    ca = _dot(hb, w_ref[:, c0 + 3 * dg:c0 + 4 * dg])
    cg = _dot(hb, w_ref[:, c0 + 4 * dg:c0 + 5 * dg])
    glu_ref[...] = ca * jax.nn.sigmoid(cg)


def _two_source_specs(tm, d, a_tiles, b_off):
    return [pl.BlockSpec((tm, d), lambda i, *_: (jnp.minimum(i, a_tiles - 1), 0)),
            pl.BlockSpec((tm, d), lambda i, *_: (jnp.maximum(i - a_tiles, 0) + b_off, 0))]


def _inproj_call(xa, xb, b_off, mod, g1, w_in_bf, qg, kg, cos, slo, shi, *, n_all, n_lat, seq):
    d = xa.shape[1]
    n_in = w_in_bf.shape[1]
    tm = TM_PROJ
    dg = (n_in - D_ATTN - 2 * D_KV) // 5
    lat_tiles = n_lat // tm
    seq_tiles = seq // tm
    n_batch = n_lat // seq

    def mod_map(i):
        return (jnp.minimum(i * tm // seq, n_batch), 0, 0)

    def rope_map(i):
        return (jnp.where(i < lat_tiles, i % seq_tiles, seq_tiles), 0)

    row = lambda i: (i, 0)
    fix = lambda i: (0, 0)
    return pl.pallas_call(
        functools.partial(_inproj_kernel, a_tiles=lat_tiles),
        grid=(n_all // tm,),
        in_specs=_two_source_specs(tm, d, lat_tiles, b_off) + [
            pl.BlockSpec((1, N_MOD, d), mod_map),
            pl.BlockSpec((1, d), fix),
            pl.BlockSpec((d, n_in), fix, pipeline_mode=pl.Buffered(1)),
            pl.BlockSpec((1, HEAD_DIM), fix),
            pl.BlockSpec((1, HEAD_DIM), fix),
            pl.BlockSpec((tm, HEAD_DIM), rope_map),
            pl.BlockSpec((tm, HEAD_DIM), rope_map),
            pl.BlockSpec((tm, HEAD_DIM), rope_map),
        ],
        out_specs=[
            pl.BlockSpec((tm, D_ATTN), row),
            pl.BlockSpec((D_KV, tm), lambda i: (0, i)),
            pl.BlockSpec((tm, D_KV), row),
            pl.BlockSpec((tm, dg), row),
            pl.BlockSpec((tm, dg), row),
            pl.BlockSpec((tm, dg), row),
        ],
        out_shape=[
            jax.ShapeDtypeStruct((n_all, D_ATTN), BF16),
            jax.ShapeDtypeStruct((D_KV, n_all), BF16),
            jax.ShapeDtypeStruct((n_all, D_KV), BF16),
            jax.ShapeDtypeStruct((n_all, dg), F32),
            jax.ShapeDtypeStruct((n_all, dg), F32),
            jax.ShapeDtypeStruct((n_all, dg), F32),
        ],
        compiler_params=_cparams("arbitrary"),
        name="in_proj",
    )(xa, xb, mod, g1, w_in_bf, qg, kg, cos, slo, shi)


def _attend(q_ref, kv_refs, g_ref, o_ref):
    outs = []
    for hh in range(N_HEADS):
        kvh = hh // GQA_GROUP
        q = q_ref[:, hh * HEAD_DIM:(hh + 1) * HEAD_DIM]
        kts = [kt_ref[kvh * HEAD_DIM:(kvh + 1) * HEAD_DIM, :] for kt_ref, _ in kv_refs]
        vs = [v_ref[:, kvh * HEAD_DIM:(kvh + 1) * HEAD_DIM] for _, v_ref in kv_refs]
        ss = [_dot(q, kt) for kt in kts]
        m = ss[0].max(axis=-1, keepdims=True)
        for s in ss[1:]:
            m = jnp.maximum(m, s.max(axis=-1, keepdims=True))
        ps = [jnp.exp2(s - m) for s in ss]
        den = ps[0].sum(axis=-1, keepdims=True)
        for p in ps[1:]:
            den = den + p.sum(axis=-1, keepdims=True)
        o = _dot(ps[0].astype(BF16), vs[0])
        for p, v in zip(ps[1:], vs[1:]):
            o = o + _dot(p.astype(BF16), v)
        outs.append(o / den)
    y = jnp.concatenate(outs, axis=-1)
    o_ref[...] = _rms(y, g_ref[...]).astype(BF16)


def _attn_kernel(q_ref, kl_ref, vl_ref, kc_ref, vc_ref, g_ref, o_ref, *, lat_steps, ctx_steps):
    if ctx_steps == 0:
        _attend(q_ref, [(kl_ref, vl_ref), (kc_ref, vc_ref)], g_ref, o_ref)
        return
    i = pl.program_id(1)

    @pl.when(i < lat_steps)
    def _():
        _attend(q_ref, [(kl_ref, vl_ref), (kc_ref, vc_ref)], g_ref, o_ref)

    @pl.when(i >= lat_steps)
    def _():
        _attend(q_ref, [(kc_ref, vc_ref)], g_ref, o_ref)


def _attn_call(q, k, v, g_attn, *, n_lat, seq, ctx_len, ctx_queries):
    tq = TQ_ATTN
    n_batch = n_lat // seq
    qt = seq // tq
    ct = ctx_len // tq if ctx_queries else 0
    n_out = n_lat + (n_batch * ctx_len if ctx_queries else 0)

    def q_map(b, i):
        return (jnp.where(i < qt, b * qt + i, n_lat // tq + b * ct + (i - qt)), 0)

    lat_v = pl.BlockSpec((seq, D_KV), lambda b, i: (b, 0))
    ctx_v = pl.BlockSpec((ctx_len, D_KV), lambda b, i: (n_lat // ctx_len + b, 0))
    lat_kt = pl.BlockSpec((D_KV, seq), lambda b, i: (0, b))
    ctx_kt = pl.BlockSpec((D_KV, ctx_len), lambda b, i: (0, n_lat // ctx_len + b))
    return pl.pallas_call(
        functools.partial(_attn_kernel, lat_steps=qt, ctx_steps=ct),
        grid=(n_batch, qt + ct),
        in_specs=[pl.BlockSpec((tq, D_ATTN), q_map), lat_kt, lat_v, ctx_kt, ctx_v,
                  pl.BlockSpec((1, D_ATTN), lambda b, i: (0, 0))],
        out_specs=pl.BlockSpec((tq, D_ATTN), q_map),
        out_shape=jax.ShapeDtypeStruct((n_out, D_ATTN), BF16),
        compiler_params=_cparams("arbitrary", "arbitrary"),
        name="attention",
    )(q, k, v, k, v, g_attn)


def _conv_taps(win, shifted, w_ref, first, width, rows):
    span = win.shape[0] - SUBLANES
    acc = None
    for j in range(SUBLANES):
        taps = [t for t in range(width) if (first + t) % SUBLANES == j]
        if not taps:
            continue
        if j == 0:
            src = win
        else:
            shifted[j - 1, :, :] = win[j:j + span, :]
            src = shifted.at[j - 1]
        for t in taps:
            a = first + t - j
            term = w_ref[t:t + 1, :] * src[a:a + rows, :]
            acc = term if acc is None else acc + term
    return acc


def _conv_kernel(gb_ref, cu_ref, cu_p_ref, cu_n_ref, gl_ref, gl_p_ref, gl_n_ref,
                 sw_ref, dw_ref, db_ref, lg_ref, lb_ref, gs_ref, gc_ref,
                 ys_ref, yc_ref, win_s, win_c, shifted, *, lat_chunks, seq_chunks, ctx_chunks):
    i = pl.program_id(0)
    r = CONV_ROWS
    hl = CONV_HALO
    pos = jnp.where(i < lat_chunks, i % seq_chunks, (i - lat_chunks) % ctx_chunks)
    last = jnp.where(i < lat_chunks, seq_chunks - 1, ctx_chunks - 1)
    keep_p = (pos > 0).astype(F32)
    keep_n = (pos < last).astype(F32)

    win_s[0:hl, :] = cu_p_ref[...] * keep_p
    win_s[hl:hl + r, :] = cu_ref[...]
    win_s[hl + r:, :] = cu_n_ref[...] * keep_n
    win_c[0:hl, :] = gl_p_ref[...] * keep_p
    win_c[hl:hl + r, :] = gl_ref[...]
    win_c[hl + r:, :] = gl_n_ref[...] * keep_n

    acc = _conv_taps(win_s, shifted, sw_ref, hl - (SCONV_WIDTH - 1) // 2, SCONV_WIDTH, r)
    ys_ref[...] = _rms(gb_ref[...] * acc, gs_ref[...]).astype(BF16)

    acc = _conv_taps(win_c, shifted, dw_ref, hl - (CONF_WIDTH - 1) // 2, CONF_WIDTH, r)
    u = acc + db_ref[...]
    mu = jnp.mean(u, axis=-1, keepdims=True)
    uc = u - mu
    var = jnp.mean(uc * uc, axis=-1, keepdims=True)
    z = uc * lax.rsqrt(var + EPS) * lg_ref[...] + lb_ref[...]
    z = z * jax.nn.sigmoid(z)
    yc_ref[...] = _rms(z, gc_ref[...]).astype(BF16)


def _conv_call(gb, cu, glu, sconv_w, dw_w, dw_b, ln_g, ln_b, g_s, g_c, *, n_rows, n_lat, seq, ctx_len):
    dg = gb.shape[1]
    r, hl = CONV_ROWS, CONV_HALO
    n_chunks = n_rows // r
    per = r // hl
    n_halo = n_rows // hl
    cur = lambda i: (i, 0)
    prev = lambda i: (jnp.maximum(i * per - 1, 0), 0)
    nxt = lambda i: (jnp.minimum((i + 1) * per, n_halo - 1), 0)
    fix = lambda i: (0, 0)
    kern = functools.partial(_conv_kernel, lat_chunks=n_lat // r, seq_chunks=seq // r,
                             ctx_chunks=max(ctx_len // r, 1))
    return pl.pallas_call(
        kern,
        grid=(n_chunks,),
        in_specs=[
            pl.BlockSpec((r, dg), cur),
            pl.BlockSpec((r, dg), cur), pl.BlockSpec((hl, dg), prev), pl.BlockSpec((hl, dg), nxt),
            pl.BlockSpec((r, dg), cur), pl.BlockSpec((hl, dg), prev), pl.BlockSpec((hl, dg), nxt),
            pl.BlockSpec((SCONV_WIDTH, dg), fix),
            pl.BlockSpec((CONF_WIDTH, dg), fix),
            pl.BlockSpec((1, dg), fix), pl.BlockSpec((1, dg), fix), pl.BlockSpec((1, dg), fix),
            pl.BlockSpec((1, dg), fix), pl.BlockSpec((1, dg), fix),
        ],
        out_specs=[pl.BlockSpec((r, dg), cur), pl.BlockSpec((r, dg), cur)],
        out_shape=[jax.ShapeDtypeStruct((n_rows, dg), BF16), jax.ShapeDtypeStruct((n_rows, dg), BF16)],
        scratch_shapes=[pltpu.VMEM((r + 2 * hl, dg), F32), pltpu.VMEM((r + 2 * hl, dg), F32),
                        pltpu.VMEM((SUBLANES - 1, r + 2 * hl - SUBLANES, dg), F32)],
        compiler_params=_cparams("arbitrary"),
        name="group_convs",
    )(gb, cu, cu, cu, glu, glu, glu, sconv_w, dw_w, dw_b, ln_g, ln_b, g_s, g_c)


def _outproj_kernel(ya_ref, ys_ref, yc_ref, xa_ref, xb_ref, mod_ref, wo_ref, g2_ref, wr_ref, br_ref,
                    xo_ref, h2_ref, lg_ref, *, a_tiles):
    da = ya_ref.shape[1]
    ds_ = ys_ref.shape[1]
    mix = _dot(ya_ref[...], wo_ref[0:da, :])
    mix = mix + _dot(ys_ref[...], wo_ref[da:da + ds_, :])
    mix = mix + _dot(yc_ref[...], wo_ref[da + ds_:, :])
    x = jnp.where(pl.program_id(0) < a_tiles, xa_ref[...], xb_ref[...])
    x = x + mod_ref[0, 2:3, :] * mix
    xo_ref[...] = x
    h2 = _rms(x, g2_ref[...]) * (1.0 + mod_ref[0, 4:5, :]) + mod_ref[0, 3:4, :]
    h2_ref[...] = h2
    lg_ref[...] = _dot(h2.astype(BF16), wr_ref[...]) + br_ref[...]


def _outproj_call(ya, ys, yc, xa, xb, b_off, mod, w_out_bf, g2, w_router, b_router, *, n_rows, n_lat, seq):
    d = xa.shape[1]
    tm = TM_PROJ
    n_batch = n_lat // seq
    row = lambda i: (i, 0)
    fix = lambda i: (0, 0)
    mod_map = lambda i: (jnp.minimum(i * tm // seq, n_batch), 0, 0)
    return pl.pallas_call(
        functools.partial(_outproj_kernel, a_tiles=n_lat // tm),
        grid=(n_rows // tm,),
        in_specs=[
            pl.BlockSpec((tm, ya.shape[1]), row),
            pl.BlockSpec((tm, ys.shape[1]), row),
            pl.BlockSpec((tm, yc.shape[1]), row),
        ] + _two_source_specs(tm, d, n_lat // tm, b_off) + [
            pl.BlockSpec((1, N_MOD, d), mod_map),
            pl.BlockSpec(w_out_bf.shape, fix, pipeline_mode=pl.Buffered(1)),
            pl.BlockSpec((1, d), fix),
            pl.BlockSpec((d, ROUTER_LANES), fix),
            pl.BlockSpec((1, ROUTER_LANES), fix),
        ],
        out_specs=[pl.BlockSpec((tm, d), row), pl.BlockSpec((tm, d), row),
                   pl.BlockSpec((tm, ROUTER_LANES), row)],
        out_shape=[jax.ShapeDtypeStruct((n_rows, d), F32), jax.ShapeDtypeStruct((n_rows, d), F32),
                   jax.ShapeDtypeStruct((n_rows, ROUTER_LANES), F32)],
        compiler_params=_cparams("arbitrary"),
        name="out_proj",
    )(ya, ys, yc, xa, xb, mod, w_out_bf, g2, w_router, b_router)


def _first_lane_of_max(vals, lane_f):
    m = vals.max(axis=-1, keepdims=True)
    return m, jnp.where(vals == m, lane_f, float(ROUTER_LANES)).min(axis=-1, keepdims=True)


def _route_kernel(lg_ref, idx_ref, wt_ref, cnt_ref, carry):
    i = pl.program_id(0)
    tr = lg_ref.shape[0]

    @pl.when(i == 0)
    def _():
        carry[...] = jnp.zeros_like(carry)

    lg = lg_ref[...]
    lane = lax.broadcasted_iota(I32, lg.shape, 1)
    lane_f = lane.astype(F32)
    neg = -jnp.inf
    gmask = lane < N_GROUPS
    gl = jnp.where(gmask, lg, neg)
    gmax, g_idx = _first_lane_of_max(gl, lane_f)
    g_top = 1.0 / jnp.where(gmask, jnp.exp(gl - gmax), 0.0).sum(axis=-1, keepdims=True)
    lo = float(N_GROUPS) + g_idx * float(EXPERTS_PER_GROUP)
    emask = jnp.logical_and(lane_f >= lo, lane_f < lo + float(EXPERTS_PER_GROUP))
    el = jnp.where(emask, lg, neg)
    m1, l1 = _first_lane_of_max(el, lane_f)
    o1 = lane_f == l1
    el2 = jnp.where(o1, neg, el)
    m2, l2 = _first_lane_of_max(el2, lane_f)
    o2 = lane_f == l2
    ratio = jnp.exp(m2 - m1)
    w1 = g_top / (1.0 + ratio)
    w2 = w1 * ratio

    onehot = jnp.where(jnp.logical_or(o1, o2), 1.0, 0.0)
    rows_i = lax.broadcasted_iota(I32, (tr, tr), 0)
    cols_i = lax.broadcasted_iota(I32, (tr, tr), 1)
    below = jnp.where(cols_i < rows_i, 1.0, 0.0).astype(BF16)
    base = carry[...] + _dot(below, onehot.astype(BF16))
    r1 = jnp.where(o1, base, 0.0).sum(axis=-1, keepdims=True)
    r2 = jnp.where(o2, base, 0.0).sum(axis=-1, keepdims=True)
    carry[...] = carry[...] + onehot.sum(axis=0, keepdims=True)
    cnt_ref[...] = jnp.broadcast_to(carry[...], cnt_ref.shape)

    e1 = l1 - float(N_GROUPS)
    e2 = l2 - float(N_GROUPS)
    idx = jnp.where(lane == 0, e1, jnp.where(lane == 1, e2, jnp.where(lane == 2, r1, r2)))
    idx_ref[...] = idx.astype(I32)
    wt_ref[...] = jnp.where(lane == 0, w1, w2)


def _route_call(logits):
    n_tok = logits.shape[0]
    tr = ROUTE_ROWS
    row = lambda i: (i, 0)
    return pl.pallas_call(
        _route_kernel,
        grid=(n_tok // tr,),
        in_specs=[pl.BlockSpec((tr, ROUTER_LANES), row)],
        out_specs=[pl.BlockSpec((tr, ROUTER_LANES), row), pl.BlockSpec((tr, ROUTER_LANES), row),
                   pl.BlockSpec((8, ROUTER_LANES), lambda i: (0, 0))],
        out_shape=[jax.ShapeDtypeStruct((n_tok, ROUTER_LANES), I32),
                   jax.ShapeDtypeStruct((n_tok, ROUTER_LANES), F32),
                   jax.ShapeDtypeStruct((8, ROUTER_LANES), F32)],
        scratch_shapes=[pltpu.VMEM((1, ROUTER_LANES), F32)],
        compiler_params=_cparams("arbitrary"),
        name="route",
    )(logits)


def _layout(idx, cnt):
    n_tok = idx.shape[0]
    rows = MOE_ROWS
    n_blocks = n_tok * TOP_K // rows + N_EXPERTS
    counts = cnt[0, N_GROUPS:N_GROUPS + N_EXPERTS].astype(I32)
    padded = (counts + rows - 1) // rows * rows
    pad_end = jnp.cumsum(padded)
    pad_start = pad_end - padded
    expert = idx[:, 0:TOP_K]
    rank = idx[:, TOP_K:2 * TOP_K]
    start = jnp.sum(jnp.where(expert[:, :, None] == jnp.arange(N_EXPERTS, dtype=I32), pad_start, 0), axis=-1)
    dest = (start + rank).reshape(-1).astype(I32)
    first_row = jnp.arange(n_blocks, dtype=I32) * rows
    block_expert = jnp.minimum(jnp.sum(pad_end[None, :] <= first_row[:, None], axis=1), N_EXPERTS - 1)
    n_used = (pad_end[-1] // rows).reshape(1)
    return dest, block_expert.astype(I32), n_used.astype(I32), pad_end.astype(I32), counts


def _dispatch_kernel(dst_ref, pend_ref, cnt_ref, h_hbm, xs_hbm, zbuf, hbuf, sem_in, sem_out, zsem):
    i = pl.program_id(0)
    n = pl.num_programs(0)
    td = DISPATCH_ROWS

    def load(step, slot):
        rows = pl.ds(pl.multiple_of(step * td, td), td)
        return pltpu.make_async_copy(h_hbm.at[rows, :], hbuf.at[slot], sem_in.at[slot])

    def drain(slot):
        for kk in range(TOP_K):
            pltpu.make_async_copy(hbuf.at[slot], xs_hbm.at[pl.ds(0, td), :], sem_out.at[slot]).wait()

    def zero_copy(e):
        start = pl.multiple_of(pend_ref[e] - MOE_ROWS, MOE_ROWS)
        return pltpu.make_async_copy(zbuf, xs_hbm.at[pl.ds(start, MOE_ROWS), :], zsem)

    @pl.when(i == 0)
    def _():
        zbuf[...] = jnp.zeros_like(zbuf)

        def start(e, c):
            @pl.when(cnt_ref[e] > 0)
            def _():
                zero_copy(e).start()
            return c

        def wait(e, c):
            @pl.when(cnt_ref[e] > 0)
            def _():
                zero_copy(e).wait()
            return c

        lax.fori_loop(0, N_EXPERTS, start, 0)
        lax.fori_loop(0, N_EXPERTS, wait, 0)

        def tail_copy(b):
            return pltpu.make_async_copy(zbuf, xs_hbm.at[pl.ds(pl.multiple_of(b * MOE_ROWS, MOE_ROWS), MOE_ROWS), :],
                                         zsem)

        n_blocks = xs_hbm.shape[0] // MOE_ROWS
        first_free = pend_ref[N_EXPERTS - 1] // MOE_ROWS
        lax.fori_loop(first_free, n_blocks, lambda b, c: (tail_copy(b).start(), c)[1], 0)
        lax.fori_loop(first_free, n_blocks, lambda b, c: (tail_copy(b).wait(), c)[1], 0)
        load(0, 0).start()

    slot = i % DISPATCH_BUFS
    nxt = (i + 1) % DISPATCH_BUFS

    @pl.when(i + 1 < n)
    def _():
        @pl.when(i + 1 >= DISPATCH_BUFS)
        def _():
            drain(nxt)
        load(i + 1, nxt).start()

    load(i, slot).wait()
    src_tile = hbuf.at[slot]
    out_sem = sem_out.at[slot]

    def group(g, c):
        base = pl.multiple_of(g * DMA_UNROLL, DMA_UNROLL)
        for u in range(DMA_UNROLL):
            src = src_tile.at[pl.ds(base + u, 1), :]
            for kk in range(TOP_K):
                row = dst_ref[TOP_K * (i * td + base + u) + kk]
                pltpu.make_async_copy(src, xs_hbm.at[pl.ds(row, 1), :], out_sem).start()
        return c

    lax.fori_loop(0, td // DMA_UNROLL, group, 0)

    @pl.when(i == n - 1)
    def _():
        for back in range(DISPATCH_BUFS):
            @pl.when(i >= back)
            def _():
                drain((i - back) % DISPATCH_BUFS)


def _dispatch_call(dest, pad_end, counts, h2, n_sorted_rows):
    n_tok, d = h2.shape
    td = DISPATCH_ROWS
    grid_spec = pltpu.PrefetchScalarGridSpec(
        num_scalar_prefetch=3,
        grid=(n_tok // td,),
        in_specs=[pl.BlockSpec(memory_space=pl.ANY)],
        out_specs=pl.BlockSpec(memory_space=pl.ANY),
        scratch_shapes=[pltpu.VMEM((MOE_ROWS, d), h2.dtype), pltpu.VMEM((DISPATCH_BUFS, td, d), h2.dtype),
                        pltpu.SemaphoreType.DMA((DISPATCH_BUFS,)), pltpu.SemaphoreType.DMA((DISPATCH_BUFS,)),
                        pltpu.SemaphoreType.DMA(())],
    )
    return pl.pallas_call(
        _dispatch_kernel,
        grid_spec=grid_spec,
        out_shape=jax.ShapeDtypeStruct((n_sorted_rows, d), h2.dtype),
        compiler_params=_cparams("arbitrary"),
        name="moe_dispatch",
    )(dest, pad_end, counts, h2)


def _moe_kernel(be_ref, used_ref, x_ref, wg_ref, wu_ref, wd_ref, y_ref, wgb, wub, wdb):
    b = pl.program_id(0)

    @pl.when(b < used_ref[0])
    def _():
        prev = be_ref[jnp.maximum(b - 1, 0)]

        @pl.when(jnp.logical_or(b == 0, be_ref[b] != prev))
        def _():
            wgb[...] = wg_ref[0, 0].astype(BF16)
            wub[...] = wu_ref[0, 0].astype(BF16)
            wdb[...] = wd_ref[0, 0].astype(BF16)

        xb = x_ref[...].astype(BF16)
        gate = _dot(xb, wgb[...])
        up = _dot(xb, wub[...])
        act = (gate * jax.nn.sigmoid(gate) * up).astype(BF16)
        y_ref[...] = _dot(act, wdb[...])

    @pl.when(b >= used_ref[0])
    def _():
        y_ref[...] = jnp.zeros_like(y_ref)


def _moe_call(x_sorted, block_expert, n_used, w_gate, w_up, w_down, layer):
    n_blocks = block_expert.shape[0]
    rows = MOE_ROWS
    _, n_exp, d, de = w_gate.shape

    def w_map(b, be, used):
        return (layer, be[jnp.minimum(b, used[0] - 1)], 0, 0)

    grid_spec = pltpu.PrefetchScalarGridSpec(
        num_scalar_prefetch=2,
        grid=(n_blocks,),
        in_specs=[
            pl.BlockSpec((rows, d), lambda b, be, used: (jnp.minimum(b, used[0] - 1), 0)),
            pl.BlockSpec((1, 1, d, de), w_map),
            pl.BlockSpec((1, 1, d, de), w_map),
            pl.BlockSpec((1, 1, de, d), w_map),
        ],
        out_specs=pl.BlockSpec((rows, d), lambda b, be, used: (b, 0)),
        scratch_shapes=[
            pltpu.VMEM((d, de), BF16),
            pltpu.VMEM((d, de), BF16),
            pltpu.VMEM((de, d), BF16),
        ],
    )
    return pl.pallas_call(
        _moe_kernel,
        grid_spec=grid_spec,
        out_shape=jax.ShapeDtypeStruct((n_blocks * rows, d), F32),
        compiler_params=_cparams("arbitrary"),
        name="moe_experts",
    )(block_expert, n_used, x_sorted, w_gate, w_up, w_down)


def _combine_kernel(dst_ref, y_hbm, x_ref, wt_ref, mod_ref, g_ref, o_ref, ybuf, sem, *, final):
    i = pl.program_id(0)
    n = pl.num_programs(0)
    rows = COMBINE_ROWS

    def gather(step, slot):
        def group(g, c):
            base = pl.multiple_of(g * DMA_UNROLL, DMA_UNROLL)
            for u in range(DMA_UNROLL):
                for kk in range(TOP_K):
                    row = dst_ref[TOP_K * (step * rows + base + u) + kk]
                    pltpu.make_async_copy(y_hbm.at[pl.ds(row, 1), :],
                                          ybuf.at[slot, pl.ds(kk * rows + base + u, 1), :], sem.at[slot]).start()
            return c
        lax.fori_loop(0, rows // DMA_UNROLL, group, 0)

    slot = i % 2

    @pl.when(i == 0)
    def _():
        gather(0, 0)

    @pl.when(i + 1 < n)
    def _():
        gather(i + 1, 1 - slot)

    pltpu.make_async_copy(y_hbm.at[pl.ds(0, TOP_K * rows), :], ybuf.at[slot], sem.at[slot]).wait()
    f = wt_ref[:, 0:1] * ybuf[slot, 0:rows, :]
    for kk in range(1, TOP_K):
        f = f + wt_ref[:, kk:kk + 1] * ybuf[slot, kk * rows:(kk + 1) * rows, :]
    x = x_ref[...] + mod_ref[0, 5:6, :] * f
    if final:
        x = _rms(x, g_ref[...])
    o_ref[...] = x


def _combine_call(dest, y_sorted, x_mid, wts, mod, g_final, *, n_rows, n_lat, seq, final):
    d = x_mid.shape[1]
    rows = COMBINE_ROWS
    n_batch = n_lat // seq
    grid_spec = pltpu.PrefetchScalarGridSpec(
        num_scalar_prefetch=1,
        grid=(n_rows // rows,),
        in_specs=[
            pl.BlockSpec(memory_space=pl.ANY),
            pl.BlockSpec((rows, d), lambda i, dst: (i, 0)),
            pl.BlockSpec((rows, ROUTER_LANES), lambda i, dst: (i, 0)),
            pl.BlockSpec((1, N_MOD, d), lambda i, dst: (jnp.minimum(i * rows // seq, n_batch), 0, 0)),
            pl.BlockSpec((1, d), lambda i, dst: (0, 0)),
        ],
        out_specs=pl.BlockSpec((rows, d), lambda i, dst: (i, 0)),
        scratch_shapes=[pltpu.VMEM((2, TOP_K * rows, d), F32), pltpu.SemaphoreType.DMA((2,))],
    )
    return pl.pallas_call(
        functools.partial(_combine_kernel, final=final),
        grid_spec=grid_spec,
        out_shape=jax.ShapeDtypeStruct((n_rows, d), F32),
        compiler_params=_cparams("arbitrary"),
        name="moe_combine",
    )(dest, y_sorted, x_mid, wts, mod, g_final)


def _rope_tables(seq, pad_rows):
    rows = seq // GRID_W
    row = jnp.broadcast_to(jnp.arange(rows, dtype=I32)[:, None], (rows, GRID_W)).reshape(-1)
    col = jnp.broadcast_to(jnp.arange(GRID_W, dtype=I32)[None, :], (rows, GRID_W)).reshape(-1)
    axis_dim = HEAD_DIM // 2
    inv_freq = ROPE_THETA ** (-jnp.arange(0, axis_dim, 2, dtype=F32) / axis_dim)
    ang_r = row.astype(F32)[:, None] * inv_freq
    ang_c = col.astype(F32)[:, None] * inv_freq
    ang = jnp.concatenate([ang_r, ang_r, ang_c, ang_c], axis=-1)
    cos, sin = jnp.cos(ang), jnp.sin(ang)
    lo = (jnp.arange(HEAD_DIM) % (HEAD_DIM // 2)) < (HEAD_DIM // 4)
    sin_lo = jnp.where(lo[None, :], -sin, 0.0)
    sin_hi = jnp.where(lo[None, :], 0.0, sin)
    cos = jnp.concatenate([cos, jnp.ones((pad_rows, HEAD_DIM), F32)], axis=0)
    zeros = jnp.zeros((pad_rows, HEAD_DIM), F32)
    return cos, jnp.concatenate([sin_lo, zeros], axis=0), jnp.concatenate([sin_hi, zeros], axis=0)


def kernel(x, c, ctx, c_ctx, w_ada, b_ada, norm1_g, w_in, q_norm_g, k_norm_g, sconv_w, conf_dw_w,
           conf_dw_b, conf_ln_g, conf_ln_b, grp_norm_g, w_out, norm2_g, router_g_w, router_g_b,
           router_e_w, router_e_b, exp_w_gate, exp_w_up, exp_w_down, final_g):
    n_batch, seq, d = x.shape
    ctx_len = ctx.shape[1]
    depth = w_ada.shape[0]
    n_lat = n_batch * seq
    n_ctx = n_batch * ctx_len
    n_all = n_lat + n_ctx
    d_sconv = sconv_w.shape[2]
    assert n_batch + 1 <= MOD_ROWS and seq % TM_PROJ == 0 and n_ctx % TM_PROJ == 0
    assert seq % CONV_ROWS == 0 and ctx_len % CONV_ROWS == 0 and seq % GRID_W == 0
    assert n_lat % DISPATCH_ROWS == 0 and n_ctx % DISPATCH_ROWS == 0 and n_lat % ctx_len == 0
    assert n_lat % COMBINE_ROWS == 0 and n_ctx % COMBINE_ROWS == 0
    assert N_GROUPS + N_EXPERTS <= ROUTER_LANES and seq % TQ_ATTN == 0 and ctx_len % TQ_ATTN == 0

    c_all = jnp.concatenate([c, c_ctx[None, :], jnp.zeros((MOD_ROWS - n_batch - 1, d), F32)], axis=0)
    mod_all = _ada_call(c_all, w_ada, b_ada).reshape(depth, MOD_ROWS, N_MOD, d)
    cos, sin_lo, sin_hi = _rope_tables(seq, TM_PROJ)
    row2 = lambda a: a.reshape(1, -1)
    lat_tiles = n_lat // TM_PROJ
    xa, xb, b_off = x.reshape(n_lat, d), ctx.reshape(n_ctx, d), 0

    for i in range(depth):
        last = i == depth - 1
        mod = mod_all[i]
        n_rows = n_lat if last else n_all
        q, k, v, gb, cu, glu = _inproj_call(
            xa, xb, b_off, mod, row2(norm1_g[i]), w_in[i].astype(BF16), row2(q_norm_g[i]),
            row2(k_norm_g[i]), cos, sin_lo, sin_hi, n_all=n_all, n_lat=n_lat, seq=seq)
        g_attn = row2(grp_norm_g[i][:D_ATTN])
        ya = _attn_call(q, k, v, g_attn, n_lat=n_lat, seq=seq, ctx_len=ctx_len, ctx_queries=not last)
        ys, yc = _conv_call(
            gb, cu, glu, sconv_w[i], conf_dw_w[i], row2(conf_dw_b[i]), row2(conf_ln_g[i]),
            row2(conf_ln_b[i]), row2(grp_norm_g[i][D_ATTN:D_ATTN + d_sconv]),
            row2(grp_norm_g[i][D_ATTN + d_sconv:]), n_rows=n_rows, n_lat=n_lat, seq=seq, ctx_len=ctx_len)
        n_pad = ROUTER_LANES - N_GROUPS - N_EXPERTS
        w_router = jnp.concatenate([router_g_w[i], router_e_w[i], jnp.zeros((d, n_pad), F32)],
                                   axis=1).astype(BF16)
        b_router = jnp.concatenate([router_g_b[i], router_e_b[i], jnp.zeros((n_pad,), F32)])[None, :]
        x_mid, h2, logits = _outproj_call(
            ya, ys, yc, xa, xb, b_off, mod, w_out[i].astype(BF16), row2(norm2_g[i]), w_router, b_router,
            n_rows=n_rows, n_lat=n_lat, seq=seq)
        idx, wts, cnt = _route_call(logits)
        dest, block_expert, n_used, pad_end, counts = _layout(idx, cnt)
        x_sorted = _dispatch_call(dest, pad_end, counts, h2, block_expert.shape[0] * MOE_ROWS)
        y_sorted = _moe_call(x_sorted, block_expert, n_used, exp_w_gate, exp_w_up, exp_w_down, i)
        x_all = _combine_call(dest, y_sorted, x_mid, wts, mod, row2(final_g),
                              n_rows=n_rows, n_lat=n_lat, seq=seq, final=last)
        xa, xb, b_off = x_all, x_all, lat_tiles
    return x_all.reshape(n_batch, seq, d)
```

```python
import functools

import jax
import jax.numpy as jnp
from jax import lax
from jax.experimental import pallas as pl
from jax.experimental.pallas import tpu as pltpu

F32 = jnp.float32
BF16 = jnp.bfloat16
I32 = jnp.int32

HEAD_DIM = 128
N_HEADS = 8
N_KV_HEADS = 2
GQA_GROUP = N_HEADS // N_KV_HEADS
D_ATTN = N_HEADS * HEAD_DIM
D_KV = N_KV_HEADS * HEAD_DIM
GRID_W = 64
ROPE_THETA = 10000.0
SCONV_WIDTH = 3
CONF_WIDTH = 31
N_GROUPS = 4
EXPERTS_PER_GROUP = 8
N_EXPERTS = N_GROUPS * EXPERTS_PER_GROUP
TOP_K = 2
N_MOD = 6
EPS = 1e-6
LOG2_E = 1.4426950408889634
SUBLANES = 8
DMA_UNROLL = 8

MOD_ROWS = 32
ROUTER_LANES = 128
TM_PROJ = 512
TQ_ATTN = 256
CONV_ROWS = 256
CONV_HALO = 16
ROUTE_ROWS = 512
DISPATCH_ROWS = 512
DISPATCH_BUFS = 3
MOE_ROWS = 512
COMBINE_ROWS = 512
ADA_TN = 1024
VMEM_LIMIT = 56 * 1024 * 1024


def _cparams(*sem):
    return pltpu.CompilerParams(dimension_semantics=sem, vmem_limit_bytes=VMEM_LIMIT)


def _dot(a, b):
    return jnp.dot(a, b, preferred_element_type=F32)


def _rms(x, g):
    return x * lax.rsqrt(jnp.mean(x * x, axis=-1, keepdims=True) + EPS) * g


def _ada_kernel(c_ref, w_ref, b_ref, o_ref):
    a = c_ref[...]
    a = a * jax.nn.sigmoid(a)
    o_ref[0] = _dot(a.astype(BF16), w_ref[0].astype(BF16)) + b_ref[0]


def _ada_call(c_all, w_ada, b_ada):
    depth, d, n = w_ada.shape
    return pl.pallas_call(
        _ada_kernel,
        grid=(depth, n // ADA_TN),
        in_specs=[
            pl.BlockSpec((MOD_ROWS, d), lambda l, j: (0, 0)),
            pl.BlockSpec((1, d, ADA_TN), lambda l, j: (l, 0, j)),
            pl.BlockSpec((1, 1, ADA_TN), lambda l, j: (l, 0, j)),
        ],
        out_specs=pl.BlockSpec((1, MOD_ROWS, ADA_TN), lambda l, j: (l, 0, j)),
        out_shape=jax.ShapeDtypeStruct((depth, MOD_ROWS, n), F32),
        compiler_params=_cparams("arbitrary", "arbitrary"),
        name="ada_mod",
    )(c_all, w_ada, b_ada.reshape(depth, 1, n))


def _rope(t, cos, sin_lo, sin_hi):
    return t * cos + pltpu.roll(t, 96, 1) * sin_lo + pltpu.roll(t, 32, 1) * sin_hi


def _inproj_kernel(xa_ref, xb_ref, mod_ref, g1_ref, w_ref, qg_ref, kg_ref, cos_ref, slo_ref, shi_ref,
                   q_ref, k_ref, v_ref, gb_ref, cu_ref, glu_ref, *, a_tiles):
    x = jnp.where(pl.program_id(0) < a_tiles, xa_ref[...], xb_ref[...])
    shift = mod_ref[0, 0:1, :]
    scale = mod_ref[0, 1:2, :]
    h = _rms(x, g1_ref[...]) * (1.0 + scale) + shift
    hb = h.astype(BF16)
    cos, slo, shi = cos_ref[...], slo_ref[...], shi_ref[...]
    qg, kg = qg_ref[...], kg_ref[...]
    qscale = HEAD_DIM ** -0.5 * LOG2_E
    half = D_ATTN // 2
    for j in range(2):
        qc = _dot(hb, w_ref[:, j * half:(j + 1) * half])
        for hh in range(half // HEAD_DIM):
            t = _rms(qc[:, hh * HEAD_DIM:(hh + 1) * HEAD_DIM], qg)
            t = _rope(t, cos, slo, shi) * qscale
            q_ref[:, j * half + hh * HEAD_DIM: j * half + (hh + 1) * HEAD_DIM] = t.astype(BF16)
    kv = _dot(hb, w_ref[:, D_ATTN:D_ATTN + 2 * D_KV])
    for hh in range(N_KV_HEADS):
        t = _rms(kv[:, hh * HEAD_DIM:(hh + 1) * HEAD_DIM], kg)
        k_ref[hh * HEAD_DIM:(hh + 1) * HEAD_DIM, :] = _rope(t, cos, slo, shi).T.astype(BF16)
    v_ref[...] = kv[:, D_KV:].astype(BF16)
    c0 = D_ATTN + 2 * D_KV
    dg = gb_ref.shape[1]
    gb_ref[...] = _dot(hb, w_ref[:, c0:c0 + dg])
    cu_ref[...] = _dot(hb, w_ref[:, c0 + dg:c0 + 2 * dg]) * _dot(hb, w_ref[:, c0 + 2 * dg:c0 + 3 * dg])
    ca = _dot(hb, w_ref[:, c0 + 3 * dg:c0 + 4 * dg])
    cg = _dot(hb, w_ref[:, c0 + 4 * dg:c0 + 5 * dg])
    glu_ref[...] = ca * jax.nn.sigmoid(cg)


def _two_source_specs(tm, d, a_tiles, b_off):
    return [pl.BlockSpec((tm, d), lambda i, *_: (jnp.minimum(i, a_tiles - 1), 0)),
            pl.BlockSpec((tm, d), lambda i, *_: (jnp.maximum(i - a_tiles, 0) + b_off, 0))]


def _inproj_call(xa, xb, b_off, mod, g1, w_in_bf, qg, kg, cos, slo, shi, *, n_all, n_lat, seq):
    d = xa.shape[1]
    n_in = w_in_bf.shape[1]
    tm = TM_PROJ
    dg = (n_in - D_ATTN - 2 * D_KV) // 5
    lat_tiles = n_lat // tm
    seq_tiles = seq // tm
    n_batch = n_lat // seq

    def mod_map(i):
        return (jnp.minimum(i * tm // seq, n_batch), 0, 0)

    def rope_map(i):
        return (jnp.where(i < lat_tiles, i % seq_tiles, seq_tiles), 0)

    row = lambda i: (i, 0)
    fix = lambda i: (0, 0)
    return pl.pallas_call(
        functools.partial(_inproj_kernel, a_tiles=lat_tiles),
        grid=(n_all // tm,),
        in_specs=_two_source_specs(tm, d, lat_tiles, b_off) + [
            pl.BlockSpec((1, N_MOD, d), mod_map),
            pl.BlockSpec((1, d), fix),
            pl.BlockSpec((d, n_in), fix, pipeline_mode=pl.Buffered(1)),
            pl.BlockSpec((1, HEAD_DIM), fix),
            pl.BlockSpec((1, HEAD_DIM), fix),
            pl.BlockSpec((tm, HEAD_DIM), rope_map),
            pl.BlockSpec((tm, HEAD_DIM), rope_map),
            pl.BlockSpec((tm, HEAD_DIM), rope_map),
        ],
        out_specs=[
            pl.BlockSpec((tm, D_ATTN), row),
            pl.BlockSpec((D_KV, tm), lambda i: (0, i)),
            pl.BlockSpec((tm, D_KV), row),
            pl.BlockSpec((tm, dg), row),
            pl.BlockSpec((tm, dg), row),
            pl.BlockSpec((tm, dg), row),
        ],
        out_shape=[
            jax.ShapeDtypeStruct((n_all, D_ATTN), BF16),
            jax.ShapeDtypeStruct((D_KV, n_all), BF16),
            jax.ShapeDtypeStruct((n_all, D_KV), BF16),
            jax.ShapeDtypeStruct((n_all, dg), F32),
            jax.ShapeDtypeStruct((n_all, dg), F32),
            jax.ShapeDtypeStruct((n_all, dg), F32),
        ],
        compiler_params=_cparams("arbitrary"),
        name="in_proj",
    )(xa, xb, mod, g1, w_in_bf, qg, kg, cos, slo, shi)


def _attend(q_ref, kv_refs, g_ref, o_ref):
    outs = []
    for hh in range(N_HEADS):
        kvh = hh // GQA_GROUP
        q = q_ref[:, hh * HEAD_DIM:(hh + 1) * HEAD_DIM]
        kts = [kt_ref[kvh * HEAD_DIM:(kvh + 1) * HEAD_DIM, :] for kt_ref, _ in kv_refs]
        vs = [v_ref[:, kvh * HEAD_DIM:(kvh + 1) * HEAD_DIM] for _, v_ref in kv_refs]
        ss = [_dot(q, kt) for kt in kts]
        m = ss[0].max(axis=-1, keepdims=True)
        for s in ss[1:]:
            m = jnp.maximum(m, s.max(axis=-1, keepdims=True))
        ps = [jnp.exp2(s - m) for s in ss]
        den = ps[0].sum(axis=-1, keepdims=True)
        for p in ps[1:]:
            den = den + p.sum(axis=-1, keepdims=True)
        o = _dot(ps[0].astype(BF16), vs[0])
        for p, v in zip(ps[1:], vs[1:]):
            o = o + _dot(p.astype(BF16), v)
        outs.append(o / den)
    y = jnp.concatenate(outs, axis=-1)
    o_ref[...] = _rms(y, g_ref[...]).astype(BF16)


def _attn_kernel(q_ref, kl_ref, vl_ref, kc_ref, vc_ref, g_ref, o_ref, *, lat_steps, ctx_steps):
    if ctx_steps == 0:
        _attend(q_ref, [(kl_ref, vl_ref), (kc_ref, vc_ref)], g_ref, o_ref)
        return
    i = pl.program_id(1)

    @pl.when(i < lat_steps)
    def _():
        _attend(q_ref, [(kl_ref, vl_ref), (kc_ref, vc_ref)], g_ref, o_ref)

    @pl.when(i >= lat_steps)
    def _():
        _attend(q_ref, [(kc_ref, vc_ref)], g_ref, o_ref)


def _attn_call(q, k, v, g_attn, *, n_lat, seq, ctx_len, ctx_queries):
    tq = TQ_ATTN
    n_batch = n_lat // seq
    qt = seq // tq
    ct = ctx_len // tq if ctx_queries else 0
    n_out = n_lat + (n_batch * ctx_len if ctx_queries else 0)

    def q_map(b, i):
        return (jnp.where(i < qt, b * qt + i, n_lat // tq + b * ct + (i - qt)), 0)

    lat_v = pl.BlockSpec((seq, D_KV), lambda b, i: (b, 0))
    ctx_v = pl.BlockSpec((ctx_len, D_KV), lambda b, i: (n_lat // ctx_len + b, 0))
    lat_kt = pl.BlockSpec((D_KV, seq), lambda b, i: (0, b))
    ctx_kt = pl.BlockSpec((D_KV, ctx_len), lambda b, i: (0, n_lat // ctx_len + b))
    return pl.pallas_call(
        functools.partial(_attn_kernel, lat_steps=qt, ctx_steps=ct),
        grid=(n_batch, qt + ct),
        in_specs=[pl.BlockSpec((tq, D_ATTN), q_map), lat_kt, lat_v, ctx_kt, ctx_v,
                  pl.BlockSpec((1, D_ATTN), lambda b, i: (0, 0))],
        out_specs=pl.BlockSpec((tq, D_ATTN), q_map),
        out_shape=jax.ShapeDtypeStruct((n_out, D_ATTN), BF16),
        compiler_params=_cparams("arbitrary", "arbitrary"),
        name="attention",
    )(q, k, v, k, v, g_attn)


def _conv_taps(win, shifted, w_ref, first, width, rows):
    span = win.shape[0] - SUBLANES
    acc = None
    for j in range(SUBLANES):
        taps = [t for t in range(width) if (first + t) % SUBLANES == j]
        if not taps:
            continue
        if j == 0:
            src = win
        else:
            shifted[j - 1, :, :] = win[j:j + span, :]
            src = shifted.at[j - 1]
        for t in taps:
            a = first + t - j
            term = w_ref[t:t + 1, :] * src[a:a + rows, :]
            acc = term if acc is None else acc + term
    return acc


def _conv_kernel(gb_ref, cu_ref, cu_p_ref, cu_n_ref, gl_ref, gl_p_ref, gl_n_ref,
                 sw_ref, dw_ref, db_ref, lg_ref, lb_ref, gs_ref, gc_ref,
                 ys_ref, yc_ref, win_s, win_c, shifted, *, lat_chunks, seq_chunks, ctx_chunks):
    i = pl.program_id(0)
    r = CONV_ROWS
    hl = CONV_HALO
    pos = jnp.where(i < lat_chunks, i % seq_chunks, (i - lat_chunks) % ctx_chunks)
    last = jnp.where(i < lat_chunks, seq_chunks - 1, ctx_chunks - 1)
    keep_p = (pos > 0).astype(F32)
    keep_n = (pos < last).astype(F32)

    win_s[0:hl, :] = cu_p_ref[...] * keep_p
    win_s[hl:hl + r, :] = cu_ref[...]
    win_s[hl + r:, :] = cu_n_ref[...] * keep_n
    win_c[0:hl, :] = gl_p_ref[...] * keep_p
    win_c[hl:hl + r, :] = gl_ref[...]
    win_c[hl + r:, :] = gl_n_ref[...] * keep_n

    acc = _conv_taps(win_s, shifted, sw_ref, hl - (SCONV_WIDTH - 1) // 2, SCONV_WIDTH, r)
    ys_ref[...] = _rms(gb_ref[...] * acc, gs_ref[...]).astype(BF16)

    acc = _conv_taps(win_c, shifted, dw_ref, hl - (CONF_WIDTH - 1) // 2, CONF_WIDTH, r)
    u = acc + db_ref[...]
    mu = jnp.mean(u, axis=-1, keepdims=True)
    uc = u - mu
    var = jnp.mean(uc * uc, axis=-1, keepdims=True)
    z = uc * lax.rsqrt(var + EPS) * lg_ref[...] + lb_ref[...]
    z = z * jax.nn.sigmoid(z)
    yc_ref[...] = _rms(z, gc_ref[...]).astype(BF16)


def _conv_call(gb, cu, glu, sconv_w, dw_w, dw_b, ln_g, ln_b, g_s, g_c, *, n_rows, n_lat, seq, ctx_len):
    dg = gb.shape[1]
    r, hl = CONV_ROWS, CONV_HALO
    n_chunks = n_rows // r
    per = r // hl
    n_halo = n_rows // hl
    cur = lambda i: (i, 0)
    prev = lambda i: (jnp.maximum(i * per - 1, 0), 0)
    nxt = lambda i: (jnp.minimum((i + 1) * per, n_halo - 1), 0)
    fix = lambda i: (0, 0)
    kern = functools.partial(_conv_kernel, lat_chunks=n_lat // r, seq_chunks=seq // r,
                             ctx_chunks=max(ctx_len // r, 1))
    return pl.pallas_call(
        kern,
        grid=(n_chunks,),
        in_specs=[
            pl.BlockSpec((r, dg), cur),
            pl.BlockSpec((r, dg), cur), pl.BlockSpec((hl, dg), prev), pl.BlockSpec((hl, dg), nxt),
            pl.BlockSpec((r, dg), cur), pl.BlockSpec((hl, dg), prev), pl.BlockSpec((hl, dg), nxt),
            pl.BlockSpec((SCONV_WIDTH, dg), fix),
            pl.BlockSpec((CONF_WIDTH, dg), fix),
            pl.BlockSpec((1, dg), fix), pl.BlockSpec((1, dg), fix), pl.BlockSpec((1, dg), fix),
            pl.BlockSpec((1, dg), fix), pl.BlockSpec((1, dg), fix),
        ],
        out_specs=[pl.BlockSpec((r, dg), cur), pl.BlockSpec((r, dg), cur)],
        out_shape=[jax.ShapeDtypeStruct((n_rows, dg), BF16), jax.ShapeDtypeStruct((n_rows, dg), BF16)],
        scratch_shapes=[pltpu.VMEM((r + 2 * hl, dg), F32), pltpu.VMEM((r + 2 * hl, dg), F32),
                        pltpu.VMEM((SUBLANES - 1, r + 2 * hl - SUBLANES, dg), F32)],
        compiler_params=_cparams("arbitrary"),
        name="group_convs",
    )(gb, cu, cu, cu, glu, glu, glu, sconv_w, dw_w, dw_b, ln_g, ln_b, g_s, g_c)


def _outproj_kernel(ya_ref, ys_ref, yc_ref, xa_ref, xb_ref, mod_ref, wo_ref, g2_ref, wr_ref, br_ref,
                    xo_ref, h2_ref, lg_ref, *, a_tiles):
    da = ya_ref.shape[1]
    ds_ = ys_ref.shape[1]
    mix = _dot(ya_ref[...], wo_ref[0:da, :])
    mix = mix + _dot(ys_ref[...], wo_ref[da:da + ds_, :])
    mix = mix + _dot(yc_ref[...], wo_ref[da + ds_:, :])
    x = jnp.where(pl.program_id(0) < a_tiles, xa_ref[...], xb_ref[...])
    x = x + mod_ref[0, 2:3, :] * mix
    xo_ref[...] = x
    h2 = _rms(x, g2_ref[...]) * (1.0 + mod_ref[0, 4:5, :]) + mod_ref[0, 3:4, :]
    h2_ref[...] = h2
    lg_ref[...] = _dot(h2.astype(BF16), wr_ref[...]) + br_ref[...]


def _outproj_call(ya, ys, yc, xa, xb, b_off, mod, w_out_bf, g2, w_router, b_router, *, n_rows, n_lat, seq):
    d = xa.shape[1]
    tm = TM_PROJ
    n_batch = n_lat // seq
    row = lambda i: (i, 0)
    fix = lambda i: (0, 0)
    mod_map = lambda i: (jnp.minimum(i * tm // seq, n_batch), 0, 0)
    return pl.pallas_call(
        functools.partial(_outproj_kernel, a_tiles=n_lat // tm),
        grid=(n_rows // tm,),
        in_specs=[
            pl.BlockSpec((tm, ya.shape[1]), row),
            pl.BlockSpec((tm, ys.shape[1]), row),
            pl.BlockSpec((tm, yc.shape[1]), row),
        ] + _two_source_specs(tm, d, n_lat // tm, b_off) + [
            pl.BlockSpec((1, N_MOD, d), mod_map),
            pl.BlockSpec(w_out_bf.shape, fix, pipeline_mode=pl.Buffered(1)),
            pl.BlockSpec((1, d), fix),
            pl.BlockSpec((d, ROUTER_LANES), fix),
            pl.BlockSpec((1, ROUTER_LANES), fix),
        ],
        out_specs=[pl.BlockSpec((tm, d), row), pl.BlockSpec((tm, d), row),
                   pl.BlockSpec((tm, ROUTER_LANES), row)],
        out_shape=[jax.ShapeDtypeStruct((n_rows, d), F32), jax.ShapeDtypeStruct((n_rows, d), F32),
                   jax.ShapeDtypeStruct((n_rows, ROUTER_LANES), F32)],
        compiler_params=_cparams("arbitrary"),
        name="out_proj",
    )(ya, ys, yc, xa, xb, mod, w_out_bf, g2, w_router, b_router)


def _first_lane_of_max(vals, lane_f):
    m = vals.max(axis=-1, keepdims=True)
    return m, jnp.where(vals == m, lane_f, float(ROUTER_LANES)).min(axis=-1, keepdims=True)


def _route_kernel(lg_ref, idx_ref, wt_ref, cnt_ref, carry):
    i = pl.program_id(0)
    tr = lg_ref.shape[0]

    @pl.when(i == 0)
    def _():
        carry[...] = jnp.zeros_like(carry)

    lg = lg_ref[...]
    lane = lax.broadcasted_iota(I32, lg.shape, 1)
    lane_f = lane.astype(F32)
    neg = -jnp.inf
    gmask = lane < N_GROUPS
    gl = jnp.where(gmask, lg, neg)
    gmax, g_idx = _first_lane_of_max(gl, lane_f)
    g_top = 1.0 / jnp.where(gmask, jnp.exp(gl - gmax), 0.0).sum(axis=-1, keepdims=True)
    lo = float(N_GROUPS) + g_idx * float(EXPERTS_PER_GROUP)
    emask = jnp.logical_and(lane_f >= lo, lane_f < lo + float(EXPERTS_PER_GROUP))
    el = jnp.where(emask, lg, neg)
    m1, l1 = _first_lane_of_max(el, lane_f)
    o1 = lane_f == l1
    el2 = jnp.where(o1, neg, el)
    m2, l2 = _first_lane_of_max(el2, lane_f)
    o2 = lane_f == l2
    ratio = jnp.exp(m2 - m1)
    w1 = g_top / (1.0 + ratio)
    w2 = w1 * ratio

    onehot = jnp.where(jnp.logical_or(o1, o2), 1.0, 0.0)
    rows_i = lax.broadcasted_iota(I32, (tr, tr), 0)
    cols_i = lax.broadcasted_iota(I32, (tr, tr), 1)
    below = jnp.where(cols_i < rows_i, 1.0, 0.0).astype(BF16)
    base = carry[...] + _dot(below, onehot.astype(BF16))
    r1 = jnp.where(o1, base, 0.0).sum(axis=-1, keepdims=True)
    r2 = jnp.where(o2, base, 0.0).sum(axis=-1, keepdims=True)
    carry[...] = carry[...] + onehot.sum(axis=0, keepdims=True)
    cnt_ref[...] = jnp.broadcast_to(carry[...], cnt_ref.shape)

    e1 = l1 - float(N_GROUPS)
    e2 = l2 - float(N_GROUPS)
    idx = jnp.where(lane == 0, e1, jnp.where(lane == 1, e2, jnp.where(lane == 2, r1, r2)))
    idx_ref[...] = idx.astype(I32)
    wt_ref[...] = jnp.where(lane == 0, w1, w2)


def _route_call(logits):
    n_tok = logits.shape[0]
    tr = ROUTE_ROWS
    row = lambda i: (i, 0)
    return pl.pallas_call(
        _route_kernel,
        grid=(n_tok // tr,),
        in_specs=[pl.BlockSpec((tr, ROUTER_LANES), row)],
        out_specs=[pl.BlockSpec((tr, ROUTER_LANES), row), pl.BlockSpec((tr, ROUTER_LANES), row),
                   pl.BlockSpec((8, ROUTER_LANES), lambda i: (0, 0))],
        out_shape=[jax.ShapeDtypeStruct((n_tok, ROUTER_LANES), I32),
                   jax.ShapeDtypeStruct((n_tok, ROUTER_LANES), F32),
                   jax.ShapeDtypeStruct((8, ROUTER_LANES), F32)],
        scratch_shapes=[pltpu.VMEM((1, ROUTER_LANES), F32)],
        compiler_params=_cparams("arbitrary"),
        name="route",
    )(logits)


def _layout(idx, cnt):
    n_tok = idx.shape[0]
    rows = MOE_ROWS
    n_blocks = n_tok * TOP_K // rows + N_EXPERTS
    counts = cnt[0, N_GROUPS:N_GROUPS + N_EXPERTS].astype(I32)
    padded = (counts + rows - 1) // rows * rows
    pad_end = jnp.cumsum(padded)
    pad_start = pad_end - padded
    expert = idx[:, 0:TOP_K]
    rank = idx[:, TOP_K:2 * TOP_K]
    start = jnp.sum(jnp.where(expert[:, :, None] == jnp.arange(N_EXPERTS, dtype=I32), pad_start, 0), axis=-1)
    dest = (start + rank).reshape(-1).astype(I32)
    first_row = jnp.arange(n_blocks, dtype=I32) * rows
    block_expert = jnp.minimum(jnp.sum(pad_end[None, :] <= first_row[:, None], axis=1), N_EXPERTS - 1)
    n_used = (pad_end[-1] // rows).reshape(1)
    return dest, block_expert.astype(I32), n_used.astype(I32), pad_end.astype(I32), counts


def _dispatch_kernel(dst_ref, pend_ref, cnt_ref, h_hbm, xs_hbm, zbuf, hbuf, sem_in, sem_out, zsem):
    i = pl.program_id(0)
    n = pl.num_programs(0)
    td = DISPATCH_ROWS

    def load(step, slot):
        rows = pl.ds(pl.multiple_of(step * td, td), td)
        return pltpu.make_async_copy(h_hbm.at[rows, :], hbuf.at[slot], sem_in.at[slot])

    def drain(slot):
        for kk in range(TOP_K):
            pltpu.make_async_copy(hbuf.at[slot], xs_hbm.at[pl.ds(0, td), :], sem_out.at[slot]).wait()

    def zero_copy(e):
        start = pl.multiple_of(pend_ref[e] - MOE_ROWS, MOE_ROWS)
        return pltpu.make_async_copy(zbuf, xs_hbm.at[pl.ds(start, MOE_ROWS), :], zsem)

    @pl.when(i == 0)
    def _():
        zbuf[...] = jnp.zeros_like(zbuf)

        def start(e, c):
            @pl.when(cnt_ref[e] > 0)
            def _():
                zero_copy(e).start()
            return c

        def wait(e, c):
            @pl.when(cnt_ref[e] > 0)
            def _():
                zero_copy(e).wait()
            return c

        lax.fori_loop(0, N_EXPERTS, start, 0)
        lax.fori_loop(0, N_EXPERTS, wait, 0)

        def tail_copy(b):
            return pltpu.make_async_copy(zbuf, xs_hbm.at[pl.ds(pl.multiple_of(b * MOE_ROWS, MOE_ROWS), MOE_ROWS), :],
                                         zsem)

        n_blocks = xs_hbm.shape[0] // MOE_ROWS
        first_free = pend_ref[N_EXPERTS - 1] // MOE_ROWS
        lax.fori_loop(first_free, n_blocks, lambda b, c: (tail_copy(b).start(), c)[1], 0)
        lax.fori_loop(first_free, n_blocks, lambda b, c: (tail_copy(b).wait(), c)[1], 0)
        load(0, 0).start()

    slot = i % DISPATCH_BUFS
    nxt = (i + 1) % DISPATCH_BUFS

    @pl.when(i + 1 < n)
    def _():
        @pl.when(i + 1 >= DISPATCH_BUFS)
        def _():
            drain(nxt)
        load(i + 1, nxt).start()

    load(i, slot).wait()
    src_tile = hbuf.at[slot]
    out_sem = sem_out.at[slot]

    def group(g, c):
        base = pl.multiple_of(g * DMA_UNROLL, DMA_UNROLL)
        for u in range(DMA_UNROLL):
            src = src_tile.at[pl.ds(base + u, 1), :]
            for kk in range(TOP_K):
                row = dst_ref[TOP_K * (i * td + base + u) + kk]
                pltpu.make_async_copy(src, xs_hbm.at[pl.ds(row, 1), :], out_sem).start(priority=kk % 2)
        return c

    lax.fori_loop(0, td // DMA_UNROLL, group, 0)

    @pl.when(i == n - 1)
    def _():
        for back in range(DISPATCH_BUFS):
            @pl.when(i >= back)
            def _():
                drain((i - back) % DISPATCH_BUFS)


def _dispatch_call(dest, pad_end, counts, h2, n_sorted_rows):
    n_tok, d = h2.shape
    td = DISPATCH_ROWS
    grid_spec = pltpu.PrefetchScalarGridSpec(
        num_scalar_prefetch=3,
        grid=(n_tok // td,),
        in_specs=[pl.BlockSpec(memory_space=pl.ANY)],
        out_specs=pl.BlockSpec(memory_space=pl.ANY),
        scratch_shapes=[pltpu.VMEM((MOE_ROWS, d), h2.dtype), pltpu.VMEM((DISPATCH_BUFS, td, d), h2.dtype),
                        pltpu.SemaphoreType.DMA((DISPATCH_BUFS,)), pltpu.SemaphoreType.DMA((DISPATCH_BUFS,)),
                        pltpu.SemaphoreType.DMA(())],
    )
    return pl.pallas_call(
        _dispatch_kernel,
        grid_spec=grid_spec,
        out_shape=jax.ShapeDtypeStruct((n_sorted_rows, d), h2.dtype),
        compiler_params=_cparams("arbitrary"),
        name="moe_dispatch",
    )(dest, pad_end, counts, h2)


def _moe_kernel(be_ref, used_ref, x_ref, wg_ref, wu_ref, wd_ref, y_ref, wgb, wub, wdb):
    b = pl.program_id(0)

    @pl.when(b < used_ref[0])
    def _():
        prev = be_ref[jnp.maximum(b - 1, 0)]

        @pl.when(jnp.logical_or(b == 0, be_ref[b] != prev))
        def _():
            wgb[...] = wg_ref[0, 0].astype(BF16)
            wub[...] = wu_ref[0, 0].astype(BF16)
            wdb[...] = wd_ref[0, 0].astype(BF16)

        xb = x_ref[...].astype(BF16)
        gate = _dot(xb, wgb[...])
        up = _dot(xb, wub[...])
        act = (gate * jax.nn.sigmoid(gate) * up).astype(BF16)
        y_ref[...] = _dot(act, wdb[...])

    @pl.when(b >= used_ref[0])
    def _():
        y_ref[...] = jnp.zeros_like(y_ref)


def _moe_call(x_sorted, block_expert, n_used, w_gate, w_up, w_down, layer):
    n_blocks = block_expert.shape[0]
    rows = MOE_ROWS
    _, n_exp, d, de = w_gate.shape

    def w_map(b, be, used):
        return (layer, be[jnp.minimum(b, used[0] - 1)], 0, 0)

    grid_spec = pltpu.PrefetchScalarGridSpec(
        num_scalar_prefetch=2,
        grid=(n_blocks,),
        in_specs=[
            pl.BlockSpec((rows, d), lambda b, be, used: (jnp.minimum(b, used[0] - 1), 0)),
            pl.BlockSpec((1, 1, d, de), w_map),
            pl.BlockSpec((1, 1, d, de), w_map),
            pl.BlockSpec((1, 1, de, d), w_map),
        ],
        out_specs=pl.BlockSpec((rows, d), lambda b, be, used: (b, 0)),
        scratch_shapes=[
            pltpu.VMEM((d, de), BF16),
            pltpu.VMEM((d, de), BF16),
            pltpu.VMEM((de, d), BF16),
        ],
    )
    return pl.pallas_call(
        _moe_kernel,
        grid_spec=grid_spec,
        out_shape=jax.ShapeDtypeStruct((n_blocks * rows, d), F32),
        compiler_params=_cparams("arbitrary"),
        name="moe_experts",
    )(block_expert, n_used, x_sorted, w_gate, w_up, w_down)


def _combine_kernel(dst_ref, y_hbm, x_ref, wt_ref, mod_ref, g_ref, o_ref, ybuf, sem, *, final):
    i = pl.program_id(0)
    n = pl.num_programs(0)
    rows = COMBINE_ROWS

    def gather(step, slot):
        def group(g, c):
            base = pl.multiple_of(g * DMA_UNROLL, DMA_UNROLL)
            for u in range(DMA_UNROLL):
                for kk in range(TOP_K):
                    row = dst_ref[TOP_K * (step * rows + base + u) + kk]
                    pltpu.make_async_copy(y_hbm.at[pl.ds(row, 1), :],
                                          ybuf.at[slot, pl.ds(kk * rows + base + u, 1), :],
                                          sem.at[slot]).start(priority=kk % 2)
            return c
        lax.fori_loop(0, rows // DMA_UNROLL, group, 0)

    slot = i % 2

    @pl.when(i == 0)
    def _():
        gather(0, 0)

    @pl.when(i + 1 < n)
    def _():
        gather(i + 1, 1 - slot)

    pltpu.make_async_copy(y_hbm.at[pl.ds(0, TOP_K * rows), :], ybuf.at[slot], sem.at[slot]).wait()
    f = wt_ref[:, 0:1] * ybuf[slot, 0:rows, :]
    for kk in range(1, TOP_K):
        f = f + wt_ref[:, kk:kk + 1] * ybuf[slot, kk * rows:(kk + 1) * rows, :]
    x = x_ref[...] + mod_ref[0, 5:6, :] * f
    if final:
        x = _rms(x, g_ref[...])
    o_ref[...] = x


def _combine_call(dest, y_sorted, x_mid, wts, mod, g_final, *, n_rows, n_lat, seq, final):
    d = x_mid.shape[1]
    rows = COMBINE_ROWS
    n_batch = n_lat // seq
    grid_spec = pltpu.PrefetchScalarGridSpec(
        num_scalar_prefetch=1,
        grid=(n_rows // rows,),
        in_specs=[
            pl.BlockSpec(memory_space=pl.ANY),
            pl.BlockSpec((rows, d), lambda i, dst: (i, 0)),
            pl.BlockSpec((rows, ROUTER_LANES), lambda i, dst: (i, 0)),
            pl.BlockSpec((1, N_MOD, d), lambda i, dst: (jnp.minimum(i * rows // seq, n_batch), 0, 0)),
            pl.BlockSpec((1, d), lambda i, dst: (0, 0)),
        ],
        out_specs=pl.BlockSpec((rows, d), lambda i, dst: (i, 0)),
        scratch_shapes=[pltpu.VMEM((2, TOP_K * rows, d), F32), pltpu.SemaphoreType.DMA((2,))],
    )
    return pl.pallas_call(
        functools.partial(_combine_kernel, final=final),
        grid_spec=grid_spec,
        out_shape=jax.ShapeDtypeStruct((n_rows, d), F32),
        compiler_params=_cparams("arbitrary"),
        name="moe_combine",
    )(dest, y_sorted, x_mid, wts, mod, g_final)


def _rope_tables(seq, pad_rows):
    rows = seq // GRID_W
    row = jnp.broadcast_to(jnp.arange(rows, dtype=I32)[:, None], (rows, GRID_W)).reshape(-1)
    col = jnp.broadcast_to(jnp.arange(GRID_W, dtype=I32)[None, :], (rows, GRID_W)).reshape(-1)
    axis_dim = HEAD_DIM // 2
    inv_freq = ROPE_THETA ** (-jnp.arange(0, axis_dim, 2, dtype=F32) / axis_dim)
    ang_r = row.astype(F32)[:, None] * inv_freq
    ang_c = col.astype(F32)[:, None] * inv_freq
    ang = jnp.concatenate([ang_r, ang_r, ang_c, ang_c], axis=-1)
    cos, sin = jnp.cos(ang), jnp.sin(ang)
    lo = (jnp.arange(HEAD_DIM) % (HEAD_DIM // 2)) < (HEAD_DIM // 4)
    sin_lo = jnp.where(lo[None, :], -sin, 0.0)
    sin_hi = jnp.where(lo[None, :], 0.0, sin)
    cos = jnp.concatenate([cos, jnp.ones((pad_rows, HEAD_DIM), F32)], axis=0)
    zeros = jnp.zeros((pad_rows, HEAD_DIM), F32)
    return cos, jnp.concatenate([sin_lo, zeros], axis=0), jnp.concatenate([sin_hi, zeros], axis=0)


def kernel(x, c, ctx, c_ctx, w_ada, b_ada, norm1_g, w_in, q_norm_g, k_norm_g, sconv_w, conf_dw_w,
           conf_dw_b, conf_ln_g, conf_ln_b, grp_norm_g, w_out, norm2_g, router_g_w, router_g_b,
           router_e_w, router_e_b, exp_w_gate, exp_w_up, exp_w_down, final_g):
    n_batch, seq, d = x.shape
    ctx_len = ctx.shape[1]
    depth = w_ada.shape[0]
    n_lat = n_batch * seq
    n_ctx = n_batch * ctx_len
    n_all = n_lat + n_ctx
    d_sconv = sconv_w.shape[2]
    assert n_batch + 1 <= MOD_ROWS and seq % TM_PROJ == 0 and n_ctx % TM_PROJ == 0
    assert seq % CONV_ROWS == 0 and ctx_len % CONV_ROWS == 0 and seq % GRID_W == 0
    assert n_lat % DISPATCH_ROWS == 0 and n_ctx % DISPATCH_ROWS == 0 and n_lat % ctx_len == 0
    assert n_lat % COMBINE_ROWS == 0 and n_ctx % COMBINE_ROWS == 0
    assert N_GROUPS + N_EXPERTS <= ROUTER_LANES and seq % TQ_ATTN == 0 and ctx_len % TQ_ATTN == 0

    c_all = jnp.concatenate([c, c_ctx[None, :], jnp.zeros((MOD_ROWS - n_batch - 1, d), F32)], axis=0)
    mod_all = _ada_call(c_all, w_ada, b_ada).reshape(depth, MOD_ROWS, N_MOD, d)
    cos, sin_lo, sin_hi = _rope_tables(seq, TM_PROJ)
    row2 = lambda a: a.reshape(1, -1)
    lat_tiles = n_lat // TM_PROJ
    xa, xb, b_off = x.reshape(n_lat, d), ctx.reshape(n_ctx, d), 0

    for i in range(depth):
        last = i == depth - 1
        mod = mod_all[i]
        n_rows = n_lat if last else n_all
        q, k, v, gb, cu, glu = _inproj_call(
            xa, xb, b_off, mod, row2(norm1_g[i]), w_in[i].astype(BF16), row2(q_norm_g[i]),
            row2(k_norm_g[i]), cos, sin_lo, sin_hi, n_all=n_all, n_lat=n_lat, seq=seq)
        g_attn = row2(grp_norm_g[i][:D_ATTN])
        ya = _attn_call(q, k, v, g_attn, n_lat=n_lat, seq=seq, ctx_len=ctx_len, ctx_queries=not last)
        ys, yc = _conv_call(
            gb, cu, glu, sconv_w[i], conf_dw_w[i], row2(conf_dw_b[i]), row2(conf_ln_g[i]),
            row2(conf_ln_b[i]), row2(grp_norm_g[i][D_ATTN:D_ATTN + d_sconv]),
            row2(grp_norm_g[i][D_ATTN + d_sconv:]), n_rows=n_rows, n_lat=n_lat, seq=seq, ctx_len=ctx_len)
        n_pad = ROUTER_LANES - N_GROUPS - N_EXPERTS
        w_router = jnp.concatenate([router_g_w[i], router_e_w[i], jnp.zeros((d, n_pad), F32)],
                                   axis=1).astype(BF16)
        b_router = jnp.concatenate([router_g_b[i], router_e_b[i], jnp.zeros((n_pad,), F32)])[None, :]
        x_mid, h2, logits = _outproj_call(
            ya, ys, yc, xa, xb, b_off, mod, w_out[i].astype(BF16), row2(norm2_g[i]), w_router, b_router,
            n_rows=n_rows, n_lat=n_lat, seq=seq)
        idx, wts, cnt = _route_call(logits)
        dest, block_expert, n_used, pad_end, counts = _layout(idx, cnt)
        x_sorted = _dispatch_call(dest, pad_end, counts, h2, block_expert.shape[0] * MOE_ROWS)
        y_sorted = _moe_call(x_sorted, block_expert, n_used, exp_w_gate, exp_w_up, exp_w_down, i)
        x_all = _combine_call(dest, y_sorted, x_mid, wts, mod, row2(final_g),
                              n_rows=n_rows, n_lat=n_lat, seq=seq, final=last)
        xa, xb, b_off = x_all, x_all, lat_tiles
    return x_all.reshape(n_batch, seq, d)
```
